```python
import jax, jax.numpy as jnp
from jax import lax
import numpy as np

D_MODEL = 2048
BATCH = 4
SEQ = 4096
DEPTH = 1

PLE_DIM = 256
D_MIX = D_MODEL
D_GLA = D_MIX // 2
D_SB = D_MIX - D_GLA
GLA_HEADS = 4
GLA_DK = (D_GLA // 2) // GLA_HEADS
GLA_DV = D_GLA // GLA_HEADS
GLA_GATE_RANK = 16
GLA_TAU = 16.0
GLA_CHUNK = 64
SB_HEADS = 8
SB_DH = D_SB // SB_HEADS
SB_BLOCK = 128
EPS = 1e-6

GLA_QK = GLA_HEADS * GLA_DK
SPLITS = [GLA_QK, GLA_QK, D_GLA, D_GLA, GLA_GATE_RANK, D_SB, D_SB, D_SB, D_SB]
D_IN = sum(SPLITS)

kernel_name = "hybrid_gla_stickbreaking_parallel_heads"


def rmsnorm(x, g):
    xf = x.astype(jnp.float32)
    return xf * lax.rsqrt(jnp.mean(xf * xf, axis=-1, keepdims=True) + EPS) * g.astype(jnp.float32)


def gla_chunked(q, k, v, log_a):
    B, S, H, dk = q.shape
    dv = v.shape[-1]
    C = GLA_CHUNK
    n = S // C

    def to_chunks(t):
        return t.astype(jnp.float32).reshape(B, n, C, H, t.shape[-1]).transpose(1, 0, 3, 2, 4)

    qc = to_chunks(q) * (dk ** -0.5)
    kc = to_chunks(k)
    vc = to_chunks(v)
    bc = jnp.cumsum(to_chunks(log_a), axis=3)
    causal = jnp.tril(jnp.ones((C, C), dtype=bool))[None, None, :, :, None]

    def step(state, inp):
        qi, ki, vi, bi = inp
        inter = jnp.einsum('bhcd,bhde->bhce', qi * jnp.exp(bi), state)
        diff = bi[:, :, :, None, :] - bi[:, :, None, :, :]
        decay = jnp.exp(jnp.where(causal, diff, -jnp.inf))
        scores = jnp.einsum('bhid,bhjd,bhijd->bhij', qi, ki, decay)
        intra = jnp.einsum('bhij,bhje->bhie', scores, vi)
        b_last = bi[:, :, -1:, :]
        new_state = state * jnp.exp(b_last)[:, :, 0, :, None] + jnp.einsum(
            'bhcd,bhce->bhde', ki * jnp.exp(b_last - bi), vi)
        return new_state, inter + intra

    s0 = jnp.zeros((B, H, dk, dv), jnp.float32)
    _, out = lax.scan(step, s0, (qc, kc, vc, bc))
    return out.transpose(1, 0, 3, 2, 4).reshape(B, S, H, dv)


def stick_breaking(q, k, v):
    S = q.shape[2]
    scale = q.shape[-1] ** -0.5
    qf, kf, vf = (t.astype(jnp.float32) for t in (q, k, v))
    outs = []
    for blk in range(S // SB_BLOCK):
        t0 = blk * SB_BLOCK
        t1 = t0 + SB_BLOCK
        z = jnp.einsum('bhtd,bhsd->bhts', qf[:, :, t0:t1], kf[:, :, :t1]) * scale
        t_idx = jnp.arange(t0, t1)[:, None]
        s_idx = jnp.arange(t1)[None, :]
        before = s_idx < t_idx
        log_keep = jnp.where(before, jax.nn.log_sigmoid(-z), 0.0)
        suffix = lax.cumsum(log_keep, axis=3, reverse=True) - log_keep
        w = jnp.where(before, jnp.exp(jax.nn.log_sigmoid(z) + suffix), 0.0)
        outs.append(jnp.einsum('bhts,bhsd->bhtd', w, vf[:, :, :t1]))
    return jnp.concatenate(outs, axis=2)


def setup_inputs(seed: int = 0) -> dict:
    key = jax.random.key(seed)
    ks = jax.random.split(key, 12)
    f32 = jnp.float32
    x = jax.random.normal(ks[0], (BATCH, SEQ, D_MODEL), f32)
    p = jax.random.normal(ks[1], (DEPTH, BATCH, SEQ, PLE_DIM), f32)
    g_pre = 1.0 + 0.02 * jax.random.normal(ks[2], (DEPTH, D_MODEL), f32)
    w_in = jax.random.normal(ks[3], (DEPTH, D_MODEL, D_IN), f32) * D_MODEL ** -0.5
    w_a2 = jax.random.normal(ks[4], (DEPTH, GLA_GATE_RANK, GLA_QK), f32) * GLA_GATE_RANK ** -0.5
    b_a = 0.1 * jax.random.normal(ks[5], (DEPTH, GLA_QK), f32)
    g_gla_head = 1.0 + 0.02 * jax.random.normal(ks[6], (DEPTH, GLA_DV), f32)
    w_out = jax.random.normal(ks[7], (DEPTH, D_MIX, D_MODEL), f32) * D_MIX ** -0.5
    g_post = 1.0 + 0.02 * jax.random.normal(ks[8], (DEPTH, D_MODEL), f32)
    w_ple_gate = jax.random.normal(ks[9], (DEPTH, D_MODEL, D_MODEL), f32) * D_MODEL ** -0.5
    b_ple_gate = 0.02 * jax.random.normal(ks[10], (DEPTH, D_MODEL), f32)
    w_ple_proj = jax.random.normal(ks[11], (DEPTH, PLE_DIM, D_MODEL), f32) * PLE_DIM ** -0.5
    return {"x": x, "p": p, "g_pre": g_pre, "w_in": w_in, "w_a2": w_a2, "b_a": b_a,
            "g_gla_head": g_gla_head, "w_out": w_out, "g_post": g_post,
            "w_ple_gate": w_ple_gate, "b_ple_gate": b_ple_gate, "w_ple_proj": w_ple_proj}


def reference(x, p, g_pre, w_in, w_a2, b_a, g_gla_head, w_out, g_post,
              w_ple_gate, b_ple_gate, w_ple_proj):
    B, S, _ = x.shape
    out_dtype = x.dtype
    h_res = x.astype(jnp.float32)
    offsets = np.cumsum(SPLITS)[:-1].tolist()
    for i in range(DEPTH):
        h = rmsnorm(h_res, g_pre[i])
        u = h @ w_in[i].astype(jnp.float32)
        (gq, gk, gv, g_gate, g_lr, sq, sk, sv, s_gate) = jnp.split(u, offsets, axis=-1)

        log_a = jax.nn.log_sigmoid(g_lr @ w_a2[i].astype(jnp.float32) + b_a[i]) / GLA_TAU
        o_gla = gla_chunked(gq.reshape(B, S, GLA_HEADS, GLA_DK),
                            gk.reshape(B, S, GLA_HEADS, GLA_DK),
                            gv.reshape(B, S, GLA_HEADS, GLA_DV),
                            log_a.reshape(B, S, GLA_HEADS, GLA_DK))
        o_gla = rmsnorm(o_gla, g_gla_head[i]).reshape(B, S, D_GLA) * jax.nn.silu(g_gate)

        def heads(t):
            return t.reshape(B, S, SB_HEADS, SB_DH).transpose(0, 2, 1, 3)
        o_sb = stick_breaking(heads(sq), heads(sk), heads(sv))
        o_sb = o_sb.transpose(0, 2, 1, 3).reshape(B, S, D_SB) * jax.nn.silu(s_gate)

        mix = jnp.concatenate([o_gla, o_sb], axis=-1) @ w_out[i].astype(jnp.float32)
        h_res = h_res + rmsnorm(mix, g_post[i])

        gate = jax.nn.sigmoid(h_res @ w_ple_gate[i].astype(jnp.float32) + b_ple_gate[i])
        h_res = h_res + gate * (p[i].astype(jnp.float32) @ w_ple_proj[i].astype(jnp.float32))
    return h_res.astype(out_dtype)
```

```python
import functools

import numpy as np
import jax
import jax.numpy as jnp
from jax import lax
from jax.experimental import pallas as pl
from jax.experimental.pallas import tpu as pltpu

F32 = jnp.float32
BF16 = jnp.bfloat16

EPS = 1e-6
GLA_HEADS = 4
GLA_DK = 128
GLA_DV = 256
GLA_RANK = 16
GLA_TAU = 16.0
GLA_CHUNK = 64
GLA_LEVELS = 6
SB_HEADS = 8
SB_DH = 128

LANES = 128
VMEM_LIMIT_BYTES = 56 * 1024 * 1024

D_QK = GLA_HEADS * GLA_DK
D_GLA = GLA_HEADS * GLA_DV
D_SB = SB_HEADS * SB_DH
U_GQ, U_GK, U_GV, U_GG = 0, D_QK, 2 * D_QK, 2 * D_QK + D_GLA
U_SQ = U_GG + D_GLA
U_SK, U_SV, U_SG = U_SQ + D_SB, U_SQ + 2 * D_SB, U_SQ + 3 * D_SB
D_U = U_SG + D_SB


def _nt_dot(a, b):
    return lax.dot_general(a, b, (((1,), (1,)), ((), ())), preferred_element_type=F32)


def _tn_dot(a, b):
    return lax.dot_general(a, b, (((0,), (0,)), ((), ())), preferred_element_type=F32)


def _dot(a, b):
    return jnp.dot(a, b, preferred_element_type=F32)


def _split_bf16(x):
    hi = x.astype(BF16)
    lo = (x - hi.astype(F32)).astype(BF16)
    return hi, lo


def _sigmoid(x):
    return 1.0 / (1.0 + jnp.exp(-x))


def _log1pexp_neg_abs(z):
    return jnp.log(1.0 + jnp.exp(-jnp.abs(z)))


def _in_proj_kernel(x_ref, g_ref, w_ref, wlr_ref, wa2_ref, ba_ref, u_ref, la_ref, h_scr):
    @pl.when(pl.program_id(1) == 0)
    def _():
        x = x_ref[...]
        ms = jnp.mean(x * x, axis=-1, keepdims=True)
        hb = (x * lax.rsqrt(ms + EPS) * g_ref[...]).astype(BF16)
        h_scr[...] = hb
        g_lr = _dot(hb, wlr_ref[...])
        z = _dot(g_lr.astype(BF16), wa2_ref[...]) + ba_ref[...]
        la_ref[...] = (jnp.minimum(z, 0.0) - _log1pexp_neg_abs(z)) * (1.0 / GLA_TAU)

    u_ref[...] = _dot(h_scr[...], w_ref[...]).astype(BF16)


def _in_proj(x2, g_pre, w_main, w_lr, w_a2, b_a, *, tm, tn):
    n, d = x2.shape
    du = w_main.shape[1]
    return pl.pallas_call(
        _in_proj_kernel,
        grid=(n // tm, du // tn),
        in_specs=[
            pl.BlockSpec((tm, d), lambda i, j: (i, 0)),
            pl.BlockSpec((1, d), lambda i, j: (0, 0)),
            pl.BlockSpec((d, tn), lambda i, j: (0, j)),
            pl.BlockSpec((d, LANES), lambda i, j: (0, 0)),
            pl.BlockSpec((LANES, D_QK), lambda i, j: (0, 0)),
            pl.BlockSpec((1, D_QK), lambda i, j: (0, 0)),
        ],
        out_specs=[
            pl.BlockSpec((tm, tn), lambda i, j: (i, j)),
            pl.BlockSpec((tm, D_QK), lambda i, j: (i, 0)),
        ],
        out_shape=[
            jax.ShapeDtypeStruct((n, du), BF16),
            jax.ShapeDtypeStruct((n, D_QK), F32),
        ],
        scratch_shapes=[pltpu.VMEM((tm, d), BF16)],
        compiler_params=pltpu.CompilerParams(
            dimension_semantics=("parallel", "arbitrary"),
            vmem_limit_bytes=VMEM_LIMIT_BYTES),
        name="in_proj",
    )(x2, g_pre, w_main, w_lr, w_a2, b_a)


def _gla_decay_matrix():
    c = GLA_CHUNK
    r = np.arange(c)[:, None]
    t = np.arange(c)[None, :]
    groups_q, groups_k = [], []
    for lvl in range(GLA_LEVELS):
        s = 1 << lvl
        start = (r // (2 * s)) * (2 * s)
        m = start + s - 1
        groups_q.append((r >= start + s) & (t > m) & (t <= r))
        groups_k.append((r <= m) & (t > r) & (t <= m))
    groups = groups_q + groups_k + [t <= r, t > r]
    return np.concatenate(groups, axis=0).astype(np.float32)


def _gla_level_matrix():
    c = GLA_CHUNK
    i = np.arange(c)[:, None]
    j = np.arange(c)[None, :]
    lvl = np.full((c, c), GLA_LEVELS + 1, np.int32)
    lvl[i == j] = GLA_LEVELS
    x = i ^ j
    for b in range(GLA_LEVELS):
        lvl[(i > j) & ((x >> b) == 1)] = b
    return lvl


def _gla_kernel(q_ref, k_ref, v_ref, gate_ref, la_ref, a_ref, lvl_ref, gh_ref, o_ref, st_ref,
                *, chunks_per_block):
    c = GLA_CHUNK

    @pl.when(pl.program_id(1) == 0)
    def _():
        st_ref[...] = jnp.zeros_like(st_ref)

    a_mat = a_ref[...]
    lvl = lvl_ref[...]
    g_head = gh_ref[...]
    scale = GLA_DK ** -0.5

    def chunk(ci, carry):
        r0 = pl.multiple_of(ci * c, c)
        rows = pl.ds(r0, c)
        la_hi, la_lo = _split_bf16(la_ref[rows, :])
        decay = jnp.exp(_dot(a_mat, la_hi) + _dot(a_mat, la_lo))
        for h in range(GLA_HEADS):
            kcols = slice(h * GLA_DK, (h + 1) * GLA_DK)
            vcols = slice(h * GLA_DV, (h + 1) * GLA_DV)
            q = q_ref[rows, kcols].astype(F32) * scale
            k = k_ref[rows, kcols].astype(F32)
            v = v_ref[rows, vcols]

            def dec(g):
                return decay[g * c:(g + 1) * c, kcols]

            scores = jnp.where(lvl == GLA_LEVELS, _nt_dot(q.astype(BF16), k.astype(BF16)), 0.0)
            for l in range(GLA_LEVELS):
                ql = (q * dec(l)).astype(BF16)
                kl = (k * dec(GLA_LEVELS + l)).astype(BF16)
                scores = jnp.where(lvl == l, _nt_dot(ql, kl), scores)
            intra = _dot(scores.astype(BF16), v)

            d_b = dec(2 * GLA_LEVELS)
            d_last = dec(2 * GLA_LEVELS + 1)
            st = st_ref[h]
            inter = _nt_dot((q * d_b).astype(BF16), st.astype(BF16))
            o = inter + intra

            kd = (k * d_last).astype(BF16)
            st_ref[h] = st * d_b[c - 1:c, :] + _tn_dot(v, kd)

            ms = jnp.mean(o * o, axis=-1, keepdims=True)
            gate = gate_ref[rows, vcols].astype(F32)
            o = o * lax.rsqrt(ms + EPS) * g_head * (gate * _sigmoid(gate))
            o_ref[rows, vcols] = o.astype(BF16)
        return carry

    lax.fori_loop(0, chunks_per_block, chunk, 0)


def _gla(u, log_a, g_head, *, batch, tc):
    n = u.shape[0]
    seq = n // batch
    nblk = seq // tc
    a_mat = jnp.asarray(_gla_decay_matrix(), BF16)
    lvl = jnp.asarray(_gla_level_matrix())
    rows = lambda b, i: b * nblk + i
    return pl.pallas_call(
        functools.partial(_gla_kernel, chunks_per_block=tc // GLA_CHUNK),
        grid=(batch, nblk),
        in_specs=[
            pl.BlockSpec((tc, D_QK), lambda b, i: (rows(b, i), U_GQ // D_QK)),
            pl.BlockSpec((tc, D_QK), lambda b, i: (rows(b, i), U_GK // D_QK)),
            pl.BlockSpec((tc, D_GLA), lambda b, i: (rows(b, i), U_GV // D_GLA)),
            pl.BlockSpec((tc, D_GLA), lambda b, i: (rows(b, i), U_GG // D_GLA)),
            pl.BlockSpec((tc, D_QK), lambda b, i: (rows(b, i), 0)),
            pl.BlockSpec(a_mat.shape, lambda b, i: (0, 0)),
            pl.BlockSpec(lvl.shape, lambda b, i: (0, 0)),
            pl.BlockSpec((1, GLA_DV), lambda b, i: (0, 0)),
        ],
        out_specs=pl.BlockSpec((tc, D_GLA), lambda b, i: (rows(b, i), 0)),
        out_shape=jax.ShapeDtypeStruct((n, D_GLA), BF16),
        scratch_shapes=[pltpu.VMEM((GLA_HEADS, GLA_DV, GLA_DK), F32)],
        compiler_params=pltpu.CompilerParams(
            dimension_semantics=("parallel", "arbitrary"),
            vmem_limit_bytes=VMEM_LIMIT_BYTES),
        name="gla",
    )(u, u, u, u, log_a, a_mat, lvl, g_head)


def _sb_kernel(q_ref, k_ref, v_ref, gate_ref, tri_ref, o_ref, *, tq):
    qi = pl.program_id(2)
    q = (q_ref[...].astype(F32) * (SB_DH ** -0.5)).astype(BF16)
    tri = tri_ref[...]
    row = lax.broadcasted_iota(jnp.int32, (tq, tq), 0)
    col = lax.broadcasted_iota(jnp.int32, (tq, tq), 1)
    before = col < row

    def step(kb, carry, acc, diagonal):
        ks = pl.ds(pl.multiple_of(kb * tq, tq), tq)
        z = _nt_dot(q, k_ref[ks, :])
        soft = _log1pexp_neg_abs(z)
        log_keep = -(jnp.maximum(z, 0.0) + soft)
        if diagonal:
            log_keep = jnp.where(before, log_keep, 0.0)
        hi, lo = _split_bf16(log_keep)
        suffix = _dot(hi, tri) + _dot(lo, tri)
        w = jnp.exp((jnp.minimum(z, 0.0) - soft) + suffix + carry)
        if diagonal:
            w = jnp.where(before, w, 0.0)
        acc = acc + _dot(w.astype(BF16), v_ref[ks, :])
        carry = carry + jnp.sum(log_keep, axis=-1, keepdims=True)
        return carry, acc

    carry = jnp.zeros((tq, 1), F32)
    acc = jnp.zeros((tq, SB_DH), F32)
    carry, acc = step(qi, carry, acc, True)

    def body(t, c):
        return step(qi - 1 - t, c[0], c[1], False)

    carry, acc = lax.fori_loop(0, qi, body, (carry, acc))
    gate = gate_ref[...].astype(F32)
    o_ref[...] = (acc * (gate * _sigmoid(gate))).astype(BF16)


def _stick_breaking(u, *, batch, tq):
    n = u.shape[0]
    seq = n // batch
    nq = seq // tq
    j = np.arange(tq)[:, None]
    s = np.arange(tq)[None, :]
    tri = jnp.asarray((j > s).astype(np.float32), BF16)
    rows = lambda b, h, i: b * nq + i
    return pl.pallas_call(
        functools.partial(_sb_kernel, tq=tq),
        grid=(batch, SB_HEADS, nq),
        in_specs=[
            pl.BlockSpec((tq, SB_DH), lambda b, h, i: (rows(b, h, i), U_SQ // SB_DH + h)),
            pl.BlockSpec((seq, SB_DH), lambda b, h, i: (b, U_SK // SB_DH + h)),
            pl.BlockSpec((seq, SB_DH), lambda b, h, i: (b, U_SV // SB_DH + h)),
            pl.BlockSpec((tq, SB_DH), lambda b, h, i: (rows(b, h, i), U_SG // SB_DH + h)),
            pl.BlockSpec((tq, tq), lambda b, h, i: (0, 0)),
        ],
        out_specs=pl.BlockSpec((tq, SB_DH), lambda b, h, i: (rows(b, h, i), h)),
        out_shape=jax.ShapeDtypeStruct((n, D_SB), BF16),
        compiler_params=pltpu.CompilerParams(
            dimension_semantics=("parallel", "parallel", "arbitrary"),
            vmem_limit_bytes=VMEM_LIMIT_BYTES),
        name="stick_breaking",
    )(u, u, u, u, tri)


def _out_kernel(og_ref, os_ref, x_ref, p_ref, wo_ref, gpost_ref, wg_ref, bg_ref, wp_ref, o_ref):
    mix = _dot(og_ref[...], wo_ref[:D_GLA, :]) + _dot(os_ref[...], wo_ref[D_GLA:, :])
    ms = jnp.mean(mix * mix, axis=-1, keepdims=True)
    h1 = x_ref[...] + mix * lax.rsqrt(ms + EPS) * gpost_ref[...]
    gate = _sigmoid(_dot(h1.astype(BF16), wg_ref[...]) + bg_ref[...])
    emb = _dot(p_ref[...].astype(BF16), wp_ref[...])
    o_ref[...] = h1 + gate * emb


def _out_proj(o_gla, o_sb, x2, p2, w_out, g_post, w_gate, b_gate, w_proj, *, tm):
    n, d = x2.shape
    dp = p2.shape[1]
    dmix = w_out.shape[0]
    const = lambda i: (0, 0)
    return pl.pallas_call(
        _out_kernel,
        grid=(n // tm,),
        in_specs=[
            pl.BlockSpec((tm, D_GLA), lambda i: (i, 0)),
            pl.BlockSpec((tm, D_SB), lambda i: (i, 0)),
            pl.BlockSpec((tm, d), lambda i: (i, 0)),
            pl.BlockSpec((tm, dp), lambda i: (i, 0)),
            pl.BlockSpec((dmix, d), const),
            pl.BlockSpec((1, d), const),
            pl.BlockSpec((d, d), const),
            pl.BlockSpec((1, d), const),
            pl.BlockSpec((dp, d), const),
        ],
        out_specs=pl.BlockSpec((tm, d), lambda i: (i, 0)),
        out_shape=jax.ShapeDtypeStruct((n, d), F32),
        compiler_params=pltpu.CompilerParams(
            dimension_semantics=("parallel",),
            vmem_limit_bytes=VMEM_LIMIT_BYTES),
        name="out_proj",
    )(o_gla, o_sb, x2, p2, w_out, g_post, w_gate, b_gate, w_proj)


def kernel(x, p, g_pre, w_in, w_a2, b_a, g_gla_head, w_out, g_post, w_ple_gate, b_ple_gate, w_ple_proj):
    batch, seq, d = x.shape
    n = batch * seq
    depth = w_in.shape[0]
    lr0 = U_SQ
    h_res = x.astype(F32).reshape(n, d)
    for i in range(depth):
        w = w_in[i]
        w_main = jnp.concatenate([w[:, :lr0], w[:, lr0 + GLA_RANK:]], axis=1).astype(BF16)
        w_lr = jnp.pad(w[:, lr0:lr0 + GLA_RANK], ((0, 0), (0, LANES - GLA_RANK))).astype(BF16)
        w_a2p = jnp.pad(w_a2[i], ((0, LANES - GLA_RANK), (0, 0))).astype(BF16)
        u, log_a = _in_proj(h_res, g_pre[i][None, :], w_main, w_lr, w_a2p, b_a[i][None, :],
                            tm=512, tn=1024)
        o_gla = _gla(u, log_a, g_gla_head[i][None, :], batch=batch, tc=512)
        o_sb = _stick_breaking(u, batch=batch, tq=256)
        h_res = _out_proj(o_gla, o_sb, h_res, p[i].astype(F32).reshape(n, -1),
                          w_out[i].astype(BF16), g_post[i][None, :],
                          w_ple_gate[i].astype(BF16), b_ple_gate[i][None, :],
                          w_ple_proj[i].astype(BF16), tm=256)
    return h_res.reshape(batch, seq, d).astype(x.dtype)
```

```python
import functools

import numpy as np
import jax
import jax.numpy as jnp
from jax import lax
from jax.experimental import pallas as pl
from jax.experimental.pallas import tpu as pltpu

F32 = jnp.float32
BF16 = jnp.bfloat16

EPS = 1e-6
GLA_HEADS = 4
GLA_DK = 128
GLA_DV = 256
GLA_RANK = 16
GLA_TAU = 16.0
GLA_CHUNK = 64
GLA_LEVELS = 6
SB_HEADS = 8
SB_DH = 128

LOG2_E = 1.4426950408889634
SB_LOG2_ZERO = -160.0

LANES = 128
VMEM_LIMIT_BYTES = 56 * 1024 * 1024

D_QK = GLA_HEADS * GLA_DK
D_GLA = GLA_HEADS * GLA_DV
D_SB = SB_HEADS * SB_DH
U_GQ, U_GK, U_GV, U_GG = 0, D_QK, 2 * D_QK, 2 * D_QK + D_GLA
U_SQ = U_GG + D_GLA
U_SK, U_SV, U_SG = U_SQ + D_SB, U_SQ + 2 * D_SB, U_SQ + 3 * D_SB
D_U = U_SG + D_SB


def _nt_dot(a, b):
    return lax.dot_general(a, b, (((1,), (1,)), ((), ())), preferred_element_type=F32)


def _tn_dot(a, b):
    return lax.dot_general(a, b, (((0,), (0,)), ((), ())), preferred_element_type=F32)


def _dot(a, b):
    return jnp.dot(a, b, preferred_element_type=F32)


def _split_bf16(x):
    hi = x.astype(BF16)
    lo = (x - hi.astype(F32)).astype(BF16)
    return hi, lo


def _sigmoid(x):
    return 1.0 / (1.0 + jnp.exp(-x))


def _log1pexp_neg_abs(z):
    return jnp.log(1.0 + jnp.exp(-jnp.abs(z)))


def _in_proj_kernel(x_ref, g_ref, w_ref, wlr_ref, wa2_ref, ba_ref, u_ref, la_ref, h_scr):
    @pl.when(pl.program_id(1) == 0)
    def _():
        x = x_ref[...]
        ms = jnp.mean(x * x, axis=-1, keepdims=True)
        hb = (x * lax.rsqrt(ms + EPS) * g_ref[...]).astype(BF16)
        h_scr[...] = hb
        g_lr = _dot(hb, wlr_ref[...])
        z = _dot(g_lr.astype(BF16), wa2_ref[...]) + ba_ref[...]
        la_ref[...] = (jnp.minimum(z, 0.0) - _log1pexp_neg_abs(z)) * (1.0 / GLA_TAU)

    u_ref[...] = _dot(h_scr[...], w_ref[...]).astype(BF16)


def _in_proj(x2, g_pre, w_main, w_lr, w_a2, b_a, *, tm, tn):
    n, d = x2.shape
    du = w_main.shape[1]
    return pl.pallas_call(
        _in_proj_kernel,
        grid=(n // tm, du // tn),
        in_specs=[
            pl.BlockSpec((tm, d), lambda i, j: (i, 0)),
            pl.BlockSpec((1, d), lambda i, j: (0, 0)),
            pl.BlockSpec((d, tn), lambda i, j: (0, j)),
            pl.BlockSpec((d, LANES), lambda i, j: (0, 0)),
            pl.BlockSpec((LANES, D_QK), lambda i, j: (0, 0)),
            pl.BlockSpec((1, D_QK), lambda i, j: (0, 0)),
        ],
        out_specs=[
            pl.BlockSpec((tm, tn), lambda i, j: (i, j)),
            pl.BlockSpec((tm, D_QK), lambda i, j: (i, 0)),
        ],
        out_shape=[
            jax.ShapeDtypeStruct((n, du), BF16),
            jax.ShapeDtypeStruct((n, D_QK), F32),
        ],
        scratch_shapes=[pltpu.VMEM((tm, d), BF16)],
        compiler_params=pltpu.CompilerParams(
            dimension_semantics=("parallel", "arbitrary"),
            vmem_limit_bytes=VMEM_LIMIT_BYTES),
        name="in_proj",
    )(x2, g_pre, w_main, w_lr, w_a2, b_a)


def _gla_decay_matrix():
    c = GLA_CHUNK
    r = np.arange(c)[:, None]
    t = np.arange(c)[None, :]
    groups_q, groups_k = [], []
    for lvl in range(GLA_LEVELS):
        s = 1 << lvl
        start = (r // (2 * s)) * (2 * s)
        m = start + s - 1
        groups_q.append((r >= start + s) & (t > m) & (t <= r))
        groups_k.append((r <= m) & (t > r) & (t <= m))
    groups = groups_q + groups_k + [t <= r, t > r]
    return np.concatenate(groups, axis=0).astype(np.float32)


def _gla_level_matrix():
    c = GLA_CHUNK
    i = np.arange(c)[:, None]
    j = np.arange(c)[None, :]
    lvl = np.full((c, c), GLA_LEVELS + 1, np.int32)
    lvl[i == j] = GLA_LEVELS
    x = i ^ j
    for b in range(GLA_LEVELS):
        lvl[(i > j) & ((x >> b) == 1)] = b
    return lvl


def _gla_kernel(q_ref, k_ref, v_ref, gate_ref, la_ref, a_ref, lvl_ref, gh_ref, o_ref, st_ref,
                *, chunks_per_block):
    c = GLA_CHUNK

    @pl.when(pl.program_id(1) == 0)
    def _():
        st_ref[...] = jnp.zeros_like(st_ref)

    a_mat = a_ref[...]
    lvl = lvl_ref[...]
    g_head = gh_ref[...]
    scale = GLA_DK ** -0.5

    def chunk(ci, carry):
        r0 = pl.multiple_of(ci * c, c)
        rows = pl.ds(r0, c)
        la_hi, la_lo = _split_bf16(la_ref[rows, :])
        decay = jnp.exp(_dot(a_mat, la_hi) + _dot(a_mat, la_lo))
        for h in range(GLA_HEADS):
            kcols = slice(h * GLA_DK, (h + 1) * GLA_DK)
            vcols = slice(h * GLA_DV, (h + 1) * GLA_DV)
            q = q_ref[rows, kcols].astype(F32) * scale
            k = k_ref[rows, kcols].astype(F32)
            v = v_ref[rows, vcols]

            def dec(g):
                return decay[g * c:(g + 1) * c, kcols]

            scores = jnp.where(lvl == GLA_LEVELS, _nt_dot(q.astype(BF16), k.astype(BF16)), 0.0)
            for l in range(GLA_LEVELS):
                ql = (q * dec(l)).astype(BF16)
                kl = (k * dec(GLA_LEVELS + l)).astype(BF16)
                scores = jnp.where(lvl == l, _nt_dot(ql, kl), scores)
            intra = _dot(scores.astype(BF16), v)

            d_b = dec(2 * GLA_LEVELS)
            d_last = dec(2 * GLA_LEVELS + 1)
            st = st_ref[h]
            inter = _nt_dot((q * d_b).astype(BF16), st.astype(BF16))
            o = inter + intra

            kd = (k * d_last).astype(BF16)
            st_ref[h] = st * d_b[c - 1:c, :] + _tn_dot(v, kd)

            ms = jnp.mean(o * o, axis=-1, keepdims=True)
            gate = gate_ref[rows, vcols].astype(F32)
            o = o * lax.rsqrt(ms + EPS) * g_head * (gate * _sigmoid(gate))
            o_ref[rows, vcols] = o.astype(BF16)
        return carry

    lax.fori_loop(0, chunks_per_block, chunk, 0)


def _gla(u, log_a, g_head, *, batch, tc):
    n = u.shape[0]
    seq = n // batch
    nblk = seq // tc
    a_mat = jnp.asarray(_gla_decay_matrix(), BF16)
    lvl = jnp.asarray(_gla_level_matrix())
    rows = lambda b, i: b * nblk + i
    return pl.pallas_call(
        functools.partial(_gla_kernel, chunks_per_block=tc // GLA_CHUNK),
        grid=(batch, nblk),
        in_specs=[
            pl.BlockSpec((tc, D_QK), lambda b, i: (rows(b, i), U_GQ // D_QK)),
            pl.BlockSpec((tc, D_QK), lambda b, i: (rows(b, i), U_GK // D_QK)),
            pl.BlockSpec((tc, D_GLA), lambda b, i: (rows(b, i), U_GV // D_GLA)),
            pl.BlockSpec((tc, D_GLA), lambda b, i: (rows(b, i), U_GG // D_GLA)),
            pl.BlockSpec((tc, D_QK), lambda b, i: (rows(b, i), 0)),
            pl.BlockSpec(a_mat.shape, lambda b, i: (0, 0)),
            pl.BlockSpec(lvl.shape, lambda b, i: (0, 0)),
            pl.BlockSpec((1, GLA_DV), lambda b, i: (0, 0)),
        ],
        out_specs=pl.BlockSpec((tc, D_GLA), lambda b, i: (rows(b, i), 0)),
        out_shape=jax.ShapeDtypeStruct((n, D_GLA), BF16),
        scratch_shapes=[pltpu.VMEM((GLA_HEADS, GLA_DV, GLA_DK), F32)],
        compiler_params=pltpu.CompilerParams(
            dimension_semantics=("parallel", "arbitrary"),
            vmem_limit_bytes=VMEM_LIMIT_BYTES),
        name="gla",
    )(u, u, u, u, log_a, a_mat, lvl, g_head)


def _sb_kernel(q_ref, k_ref, v_ref, gate_ref, tri_ref, o_ref, carry_ref, acc_ref, *, tq):
    qi = pl.program_id(2)
    qn = (q_ref[...].astype(F32) * (-(SB_DH ** -0.5) * LOG2_E)).astype(BF16)
    tri = tri_ref[...]
    row = lax.broadcasted_iota(jnp.int32, (tq, tq), 0)
    col = lax.broadcasted_iota(jnp.int32, (tq, tq), 1)
    before = col < row

    def block(kb, carry, diagonal):
        ks = pl.ds(pl.multiple_of(kb * tq, tq), tq)
        y = _nt_dot(qn, k_ref[ks, :])
        soft = jnp.log2(1.0 + jnp.exp2(-jnp.abs(y)))
        log_keep = jnp.minimum(y, 0.0) - soft
        if diagonal:
            log_keep = jnp.where(before, log_keep, 0.0)
        hi, lo = _split_bf16(log_keep)
        suffix = _dot(hi, tri) + _dot(lo, tri)
        w = jnp.exp2((log_keep - y) + suffix + carry)
        if diagonal:
            w = jnp.where(before, w, 0.0)
        out = _dot(w.astype(BF16), v_ref[ks, :])
        return carry + jnp.sum(log_keep, axis=-1, keepdims=True), out

    zero = jnp.zeros((tq, 1), F32)

    @pl.when(qi == 0)
    def _():
        _, acc_ref[...] = block(0, zero, True)
        carry_ref[...] = jnp.full((tq, 1), SB_LOG2_ZERO, F32)

    @pl.when(qi > 0)
    def _():
        c1, a1 = block(qi, zero, True)
        c2, a2 = block(qi - 1, c1, False)
        carry_ref[...] = c2
        acc_ref[...] = a1 + a2

    def more(state):
        kb, live = state
        return jnp.logical_and(kb >= 0, live)

    def body(state):
        kb, _ = state
        c, a = block(kb, carry_ref[...], False)
        carry_ref[...] = c
        acc_ref[...] += a
        return kb - 1, jnp.max(c) > SB_LOG2_ZERO

    lax.while_loop(more, body, (qi - 2, jnp.max(carry_ref[...]) > SB_LOG2_ZERO))
    gate = gate_ref[...].astype(F32)
    o_ref[...] = (acc_ref[...] * (gate * _sigmoid(gate))).astype(BF16)


def _stick_breaking(u, *, batch, tq):
    n = u.shape[0]
    seq = n // batch
    nq = seq // tq
    j = np.arange(tq)[:, None]
    s = np.arange(tq)[None, :]
    tri = jnp.asarray((j > s).astype(np.float32), BF16)
    rows = lambda b, h, i: b * nq + i
    return pl.pallas_call(
        functools.partial(_sb_kernel, tq=tq),
        grid=(batch, SB_HEADS, nq),
        in_specs=[
            pl.BlockSpec((tq, SB_DH), lambda b, h, i: (rows(b, h, i), U_SQ // SB_DH + h)),
            pl.BlockSpec((seq, SB_DH), lambda b, h, i: (b, U_SK // SB_DH + h)),
            pl.BlockSpec((seq, SB_DH), lambda b, h, i: (b, U_SV // SB_DH + h)),
            pl.BlockSpec((tq, SB_DH), lambda b, h, i: (rows(b, h, i), U_SG // SB_DH + h)),
            pl.BlockSpec((tq, tq), lambda b, h, i: (0, 0)),
        ],
        out_specs=pl.BlockSpec((tq, SB_DH), lambda b, h, i: (rows(b, h, i), h)),
        out_shape=jax.ShapeDtypeStruct((n, D_SB), BF16),
        scratch_shapes=[pltpu.VMEM((tq, 1), F32), pltpu.VMEM((tq, SB_DH), F32)],
        compiler_params=pltpu.CompilerParams(
            dimension_semantics=("parallel", "parallel", "arbitrary"),
            vmem_limit_bytes=VMEM_LIMIT_BYTES),
        name="stick_breaking",
    )(u, u, u, u, tri)


def _out_kernel(og_ref, os_ref, x_ref, p_ref, wo_ref, gpost_ref, wg_ref, bg_ref, wp_ref, o_ref):
    mix = _dot(og_ref[...], wo_ref[:D_GLA, :]) + _dot(os_ref[...], wo_ref[D_GLA:, :])
    ms = jnp.mean(mix * mix, axis=-1, keepdims=True)
    h1 = x_ref[...] + mix * lax.rsqrt(ms + EPS) * gpost_ref[...]
    gate = _sigmoid(_dot(h1.astype(BF16), wg_ref[...]) + bg_ref[...])
    emb = _dot(p_ref[...].astype(BF16), wp_ref[...])
    o_ref[...] = h1 + gate * emb


def _out_proj(o_gla, o_sb, x2, p2, w_out, g_post, w_gate, b_gate, w_proj, *, tm):
    n, d = x2.shape
    dp = p2.shape[1]
    dmix = w_out.shape[0]
    const = lambda i: (0, 0)
    return pl.pallas_call(
        _out_kernel,
        grid=(n // tm,),
        in_specs=[
            pl.BlockSpec((tm, D_GLA), lambda i: (i, 0)),
            pl.BlockSpec((tm, D_SB), lambda i: (i, 0)),
            pl.BlockSpec((tm, d), lambda i: (i, 0)),
            pl.BlockSpec((tm, dp), lambda i: (i, 0)),
            pl.BlockSpec((dmix, d), const),
            pl.BlockSpec((1, d), const),
            pl.BlockSpec((d, d), const),
            pl.BlockSpec((1, d), const),
            pl.BlockSpec((dp, d), const),
        ],
        out_specs=pl.BlockSpec((tm, d), lambda i: (i, 0)),
        out_shape=jax.ShapeDtypeStruct((n, d), F32),
        compiler_params=pltpu.CompilerParams(
            dimension_semantics=("parallel",),
            vmem_limit_bytes=VMEM_LIMIT_BYTES),
        name="out_proj",
    )(o_gla, o_sb, x2, p2, w_out, g_post, w_gate, b_gate, w_proj)


def kernel(x, p, g_pre, w_in, w_a2, b_a, g_gla_head, w_out, g_post, w_ple_gate, b_ple_gate, w_ple_proj):
    batch, seq, d = x.shape
    n = batch * seq
    depth = w_in.shape[0]
    lr0 = U_SQ
    h_res = x.astype(F32).reshape(n, d)
    for i in range(depth):
        w = w_in[i]
        w_main = jnp.concatenate([w[:, :lr0], w[:, lr0 + GLA_RANK:]], axis=1).astype(BF16)
        w_lr = jnp.pad(w[:, lr0:lr0 + GLA_RANK], ((0, 0), (0, LANES - GLA_RANK))).astype(BF16)
        w_a2p = jnp.pad(w_a2[i], ((0, LANES - GLA_RANK), (0, 0))).astype(BF16)
        u, log_a = _in_proj(h_res, g_pre[i][None, :], w_main, w_lr, w_a2p, b_a[i][None, :],
                            tm=512, tn=1024)
        o_gla = _gla(u, log_a, g_gla_head[i][None, :], batch=batch, tc=512)
        o_sb = _stick_breaking(u, batch=batch, tq=256)
        h_res = _out_proj(o_gla, o_sb, h_res, p[i].astype(F32).reshape(n, -1),
                          w_out[i].astype(BF16), g_post[i][None, :],
                          w_ple_gate[i].astype(BF16), b_ple_gate[i][None, :],
                          w_ple_proj[i].astype(BF16), tm=256)
    return h_res.reshape(batch, seq, d).astype(x.dtype)
```

```python
import functools

import numpy as np
import jax
import jax.numpy as jnp
from jax import lax
from jax.experimental import pallas as pl
from jax.experimental.pallas import tpu as pltpu

F32 = jnp.float32
BF16 = jnp.bfloat16

EPS = 1e-6
GLA_HEADS = 4
GLA_DK = 128
GLA_DV = 256
GLA_RANK = 16
GLA_TAU = 16.0
GLA_CHUNK = 64
GLA_LEVELS = 6
SB_HEADS = 8
SB_DH = 128

LOG2_E = 1.4426950408889634
SB_LOG2_ZERO = -160.0

LANES = 128
VMEM_LIMIT_BYTES = 56 * 1024 * 1024

D_QK = GLA_HEADS * GLA_DK
D_GLA = GLA_HEADS * GLA_DV
D_SB = SB_HEADS * SB_DH
U_GQ, U_GK, U_GV, U_GG = 0, D_QK, 2 * D_QK, 2 * D_QK + D_GLA
U_SQ = U_GG + D_GLA
U_SK, U_SV, U_SG = U_SQ + D_SB, U_SQ + 2 * D_SB, U_SQ + 3 * D_SB
D_U = U_SG + D_SB


def _nt_dot(a, b):
    return lax.dot_general(a, b, (((1,), (1,)), ((), ())), preferred_element_type=F32)


def _tn_dot(a, b):
    return lax.dot_general(a, b, (((0,), (0,)), ((), ())), preferred_element_type=F32)


def _dot(a, b):
    return jnp.dot(a, b, preferred_element_type=F32)


def _split_bf16(x):
    hi = x.astype(BF16)
    lo = (x - hi.astype(F32)).astype(BF16)
    return hi, lo


def _sigmoid(x):
    return 1.0 / (1.0 + jnp.exp(-x))


def _log1pexp_neg_abs(z):
    return jnp.log(1.0 + jnp.exp(-jnp.abs(z)))


def _in_proj_kernel(x_ref, g_ref, w_ref, wlr_ref, wa2_ref, ba_ref, u_ref, la_ref, h_scr):
    @pl.when(pl.program_id(1) == 0)
    def _():
        x = x_ref[...]
        ms = jnp.mean(x * x, axis=-1, keepdims=True)
        hb = (x * lax.rsqrt(ms + EPS) * g_ref[...]).astype(BF16)
        h_scr[...] = hb
        g_lr = _dot(hb, wlr_ref[...])
        z = _dot(g_lr.astype(BF16), wa2_ref[...]) + ba_ref[...]
        la_ref[...] = (jnp.minimum(z, 0.0) - _log1pexp_neg_abs(z)) * (1.0 / GLA_TAU)

    u_ref[...] = _dot(h_scr[...], w_ref[...]).astype(BF16)


def _in_proj(x2, g_pre, w_main, w_lr, w_a2, b_a, *, tm, tn):
    n, d = x2.shape
    du = w_main.shape[1]
    return pl.pallas_call(
        _in_proj_kernel,
        grid=(n // tm, du // tn),
        in_specs=[
            pl.BlockSpec((tm, d), lambda i, j: (i, 0)),
            pl.BlockSpec((1, d), lambda i, j: (0, 0)),
            pl.BlockSpec((d, tn), lambda i, j: (0, j)),
            pl.BlockSpec((d, LANES), lambda i, j: (0, 0)),
            pl.BlockSpec((LANES, D_QK), lambda i, j: (0, 0)),
            pl.BlockSpec((1, D_QK), lambda i, j: (0, 0)),
        ],
        out_specs=[
            pl.BlockSpec((tm, tn), lambda i, j: (i, j)),
            pl.BlockSpec((tm, D_QK), lambda i, j: (i, 0)),
        ],
        out_shape=[
            jax.ShapeDtypeStruct((n, du), BF16),
            jax.ShapeDtypeStruct((n, D_QK), F32),
        ],
        scratch_shapes=[pltpu.VMEM((tm, d), BF16)],
        compiler_params=pltpu.CompilerParams(
            dimension_semantics=("parallel", "arbitrary"),
            vmem_limit_bytes=VMEM_LIMIT_BYTES),
        name="in_proj",
    )(x2, g_pre, w_main, w_lr, w_a2, b_a)


def _gla_decay_matrix():
    c = GLA_CHUNK
    r = np.arange(c)[:, None]
    t = np.arange(c)[None, :]
    groups = []
    for lvl in range(GLA_LEVELS):
        s = 1 << lvl
        start = (r // (2 * s)) * (2 * s)
        m = start + s - 1
        upper = (r >= start + s) & (t > m) & (t <= r)
        lower = (r <= m) & (t > r) & (t <= m)
        groups.append(upper | lower)
    groups += [t <= r, t > r]
    a = np.concatenate(groups, axis=0).astype(np.float32)
    return np.concatenate([a, a], axis=1)


def _gla_level_matrix():
    c = GLA_CHUNK
    i = np.arange(c)[:, None]
    j = np.arange(c)[None, :]
    lvl = np.full((c, c), GLA_LEVELS + 1, np.int32)
    lvl[i == j] = GLA_LEVELS
    x = i ^ j
    for b in range(GLA_LEVELS):
        lvl[(i > j) & ((x >> b) == 1)] = b
    return lvl


def _gla_kernel(q_ref, k_ref, v_ref, gate_ref, la_ref, a_ref, lvl_ref, gh_ref, o_ref, st_ref,
                *, chunks_per_block):
    c = GLA_CHUNK

    @pl.when(pl.program_id(1) == 0)
    def _():
        st_ref[...] = jnp.zeros_like(st_ref)

    a_mat = a_ref[...]
    lvl = lvl_ref[...]
    g_head = gh_ref[...]
    scale = GLA_DK ** -0.5

    def chunk(ci, carry):
        r0 = pl.multiple_of(ci * c, c)
        rows = pl.ds(r0, c)
        la_parts = jnp.concatenate(_split_bf16(la_ref[rows, :]), axis=0)
        decay = jnp.exp(_dot(a_mat, la_parts))
        for h in range(GLA_HEADS):
            kcols = slice(h * GLA_DK, (h + 1) * GLA_DK)
            vcols = slice(h * GLA_DV, (h + 1) * GLA_DV)
            q = q_ref[rows, kcols].astype(F32) * scale
            k = k_ref[rows, kcols].astype(F32)
            v = v_ref[rows, vcols]

            def dec(g):
                return decay[g * c:(g + 1) * c, kcols]

            scores = jnp.where(lvl == GLA_LEVELS, _nt_dot(q.astype(BF16), k.astype(BF16)), 0.0)
            for l in range(GLA_LEVELS):
                ql = (q * dec(l)).astype(BF16)
                kl = (k * dec(l)).astype(BF16)
                scores = jnp.where(lvl == l, _nt_dot(ql, kl), scores)
            intra = _dot(scores.astype(BF16), v)

            d_b = dec(GLA_LEVELS)
            d_last = dec(GLA_LEVELS + 1)
            st = st_ref[h]
            inter = _nt_dot((q * d_b).astype(BF16), st.astype(BF16))
            o = inter + intra

            kd = (k * d_last).astype(BF16)
            st_ref[h] = st * d_b[c - 1:c, :] + _tn_dot(v, kd)

            ms = jnp.mean(o * o, axis=-1, keepdims=True)
            gate = gate_ref[rows, vcols].astype(F32)
            o = o * lax.rsqrt(ms + EPS) * g_head * (gate * _sigmoid(gate))
            o_ref[rows, vcols] = o.astype(BF16)
        return carry

    lax.fori_loop(0, chunks_per_block, chunk, 0, unroll=4)


def _gla(u, log_a, g_head, *, batch, tc):
    n = u.shape[0]
    seq = n // batch
    nblk = seq // tc
    a_mat = jnp.asarray(_gla_decay_matrix(), BF16)
    lvl = jnp.asarray(_gla_level_matrix())
    rows = lambda b, i: b * nblk + i
    return pl.pallas_call(
        functools.partial(_gla_kernel, chunks_per_block=tc // GLA_CHUNK),
        grid=(batch, nblk),
        in_specs=[
            pl.BlockSpec((tc, D_QK), lambda b, i: (rows(b, i), U_GQ // D_QK)),
            pl.BlockSpec((tc, D_QK), lambda b, i: (rows(b, i), U_GK // D_QK)),
            pl.BlockSpec((tc, D_GLA), lambda b, i: (rows(b, i), U_GV // D_GLA)),
            pl.BlockSpec((tc, D_GLA), lambda b, i: (rows(b, i), U_GG // D_GLA)),
            pl.BlockSpec((tc, D_QK), lambda b, i: (rows(b, i), 0)),
            pl.BlockSpec(a_mat.shape, lambda b, i: (0, 0)),
            pl.BlockSpec(lvl.shape, lambda b, i: (0, 0)),
            pl.BlockSpec((1, GLA_DV), lambda b, i: (0, 0)),
        ],
        out_specs=pl.BlockSpec((tc, D_GLA), lambda b, i: (rows(b, i), 0)),
        out_shape=jax.ShapeDtypeStruct((n, D_GLA), BF16),
        scratch_shapes=[pltpu.VMEM((GLA_HEADS, GLA_DV, GLA_DK), F32)],
        compiler_params=pltpu.CompilerParams(
            dimension_semantics=("parallel", "arbitrary"),
            vmem_limit_bytes=VMEM_LIMIT_BYTES),
        name="gla",
    )(u, u, u, u, log_a, a_mat, lvl, g_head)


def _sb_kernel(q_ref, k_ref, v_ref, gate_ref, tri_ref, o_ref, carry_ref, acc_ref, *, tq, heads):
    qi = pl.program_id(2)
    tri = tri_ref[...]
    row = lax.broadcasted_iota(jnp.int32, (tq, tq), 0)
    col = lax.broadcasted_iota(jnp.int32, (tq, tq), 1)
    before = col < row
    hcols = [slice(h * SB_DH, (h + 1) * SB_DH) for h in range(heads)]
    qn = [(q_ref[:, c].astype(F32) * (-(SB_DH ** -0.5) * LOG2_E)).astype(BF16) for c in hcols]

    def block(h, kb, carry, diagonal):
        ks = pl.ds(pl.multiple_of(kb * tq, tq), tq)
        y = _nt_dot(qn[h], k_ref[ks, hcols[h]])
        soft = jnp.log2(1.0 + jnp.exp2(-jnp.abs(y)))
        log_keep = jnp.minimum(y, 0.0) - soft
        if diagonal:
            log_keep = jnp.where(before, log_keep, 0.0)
        parts = jnp.concatenate(_split_bf16(log_keep), axis=1)
        suffix = _dot(parts, tri)
        w = jnp.exp2((log_keep - y) + suffix + carry)
        if diagonal:
            w = jnp.where(before, w, 0.0)
        out = _dot(w.astype(BF16), v_ref[ks, hcols[h]])
        return carry + jnp.sum(log_keep, axis=-1, keepdims=True), out

    zero = jnp.zeros((tq, 1), F32)

    @pl.when(qi == 0)
    def _():
        for h in range(heads):
            _, acc_ref[:, hcols[h]] = block(h, 0, zero, True)
        carry_ref[...] = jnp.full(carry_ref.shape, SB_LOG2_ZERO, F32)

    @pl.when(qi > 0)
    def _():
        for h in range(heads):
            c1, a1 = block(h, qi, zero, True)
            c2, a2 = block(h, qi - 1, c1, False)
            carry_ref[h] = c2
            acc_ref[:, hcols[h]] = a1 + a2

    def more(state):
        kb, live = state
        return jnp.logical_and(kb >= 0, live)

    def body(state):
        kb, _ = state
        top = None
        for h in range(heads):
            c, a = block(h, kb, carry_ref[h], False)
            carry_ref[h] = c
            acc_ref[:, hcols[h]] += a
            top = jnp.max(c) if top is None else jnp.maximum(top, jnp.max(c))
        return kb - 1, top > SB_LOG2_ZERO

    lax.while_loop(more, body, (qi - 2, jnp.max(carry_ref[...]) > SB_LOG2_ZERO))
    gate = gate_ref[...].astype(F32)
    o_ref[...] = (acc_ref[...] * (gate * _sigmoid(gate))).astype(BF16)


def _stick_breaking(u, *, batch, tq, heads):
    n = u.shape[0]
    seq = n // batch
    nq = seq // tq
    j = np.arange(tq)[:, None]
    s = np.arange(tq)[None, :]
    tri = jnp.asarray(np.tile((j > s).astype(np.float32), (2, 1)), BF16)
    width = heads * SB_DH
    rows = lambda b, g, i: b * nq + i
    return pl.pallas_call(
        functools.partial(_sb_kernel, tq=tq, heads=heads),
        grid=(batch, SB_HEADS // heads, nq),
        in_specs=[
            pl.BlockSpec((tq, width), lambda b, g, i: (rows(b, g, i), U_SQ // width + g)),
            pl.BlockSpec((seq, width), lambda b, g, i: (b, U_SK // width + g)),
            pl.BlockSpec((seq, width), lambda b, g, i: (b, U_SV // width + g)),
            pl.BlockSpec((tq, width), lambda b, g, i: (rows(b, g, i), U_SG // width + g)),
            pl.BlockSpec((2 * tq, tq), lambda b, g, i: (0, 0)),
        ],
        out_specs=pl.BlockSpec((tq, width), lambda b, g, i: (rows(b, g, i), g)),
        out_shape=jax.ShapeDtypeStruct((n, D_SB), BF16),
        scratch_shapes=[pltpu.VMEM((heads, tq, 1), F32), pltpu.VMEM((tq, width), F32)],
        compiler_params=pltpu.CompilerParams(
            dimension_semantics=("parallel", "parallel", "arbitrary"),
            vmem_limit_bytes=VMEM_LIMIT_BYTES),
        name="stick_breaking",
    )(u, u, u, u, tri)


def _out_kernel(og_ref, os_ref, x_ref, p_ref, wo_ref, gpost_ref, wg_ref, bg_ref, wp_ref, o_ref):
    mix = _dot(og_ref[...], wo_ref[:D_GLA, :]) + _dot(os_ref[...], wo_ref[D_GLA:, :])
    ms = jnp.mean(mix * mix, axis=-1, keepdims=True)
    h1 = x_ref[...] + mix * lax.rsqrt(ms + EPS) * gpost_ref[...]
    gate = _sigmoid(_dot(h1.astype(BF16), wg_ref[...]) + bg_ref[...])
    emb = _dot(p_ref[...].astype(BF16), wp_ref[...])
    o_ref[...] = h1 + gate * emb


def _out_proj(o_gla, o_sb, x2, p2, w_out, g_post, w_gate, b_gate, w_proj, *, tm):
    n, d = x2.shape
    dp = p2.shape[1]
    dmix = w_out.shape[0]
    const = lambda i: (0, 0)
    return pl.pallas_call(
        _out_kernel,
        grid=(n // tm,),
        in_specs=[
            pl.BlockSpec((tm, D_GLA), lambda i: (i, 0)),
            pl.BlockSpec((tm, D_SB), lambda i: (i, 0)),
            pl.BlockSpec((tm, d), lambda i: (i, 0)),
            pl.BlockSpec((tm, dp), lambda i: (i, 0)),
            pl.BlockSpec((dmix, d), const),
            pl.BlockSpec((1, d), const),
            pl.BlockSpec((d, d), const),
            pl.BlockSpec((1, d), const),
            pl.BlockSpec((dp, d), const),
        ],
        out_specs=pl.BlockSpec((tm, d), lambda i: (i, 0)),
        out_shape=jax.ShapeDtypeStruct((n, d), F32),
        compiler_params=pltpu.CompilerParams(
            dimension_semantics=("parallel",),
            vmem_limit_bytes=VMEM_LIMIT_BYTES),
        name="out_proj",
    )(o_gla, o_sb, x2, p2, w_out, g_post, w_gate, b_gate, w_proj)


def kernel(x, p, g_pre, w_in, w_a2, b_a, g_gla_head, w_out, g_post, w_ple_gate, b_ple_gate, w_ple_proj):
    batch, seq, d = x.shape
    n = batch * seq
    depth = w_in.shape[0]
    lr0 = U_SQ
    h_res = x.astype(F32).reshape(n, d)
    for i in range(depth):
        w = w_in[i]
        w_main = jnp.concatenate([w[:, :lr0], w[:, lr0 + GLA_RANK:]], axis=1).astype(BF16)
        w_lr = jnp.pad(w[:, lr0:lr0 + GLA_RANK], ((0, 0), (0, LANES - GLA_RANK))).astype(BF16)
        w_a2p = jnp.pad(w_a2[i], ((0, LANES - GLA_RANK), (0, 0))).astype(BF16)
        u, log_a = _in_proj(h_res, g_pre[i][None, :], w_main, w_lr, w_a2p, b_a[i][None, :],
                            tm=1024, tn=1024)
        o_gla = _gla(u, log_a, g_gla_head[i][None, :], batch=batch, tc=512)
        o_sb = _stick_breaking(u, batch=batch, tq=256, heads=8)
        h_res = _out_proj(o_gla, o_sb, h_res, p[i].astype(F32).reshape(n, -1),
                          w_out[i].astype(BF16), g_post[i][None, :],
                          w_ple_gate[i].astype(BF16), b_ple_gate[i][None, :],
                          w_ple_proj[i].astype(BF16), tm=256)
    return h_res.reshape(batch, seq, d).astype(x.dtype)
```

```python
import functools

import numpy as np
import jax
import jax.numpy as jnp
from jax import lax
from jax.experimental import pallas as pl
from jax.experimental.pallas import tpu as pltpu

F32 = jnp.float32
BF16 = jnp.bfloat16

EPS = 1e-6
GLA_HEADS = 4
GLA_DK = 128
GLA_DV = 256
GLA_RANK = 16
GLA_TAU = 16.0
GLA_CHUNK = 64
GLA_LEVELS = 6
SB_HEADS = 8
SB_DH = 128

LOG2_E = 1.4426950408889634
SB_LOG2_ZERO = -160.0

LANES = 128
VMEM_LIMIT_BYTES = 56 * 1024 * 1024

D_QK = GLA_HEADS * GLA_DK
D_GLA = GLA_HEADS * GLA_DV
D_SB = SB_HEADS * SB_DH
U_GQ, U_GK, U_GV, U_GG = 0, D_QK, 2 * D_QK, 2 * D_QK + D_GLA
U_SQ = U_GG + D_GLA
U_SK, U_SV, U_SG = U_SQ + D_SB, U_SQ + 2 * D_SB, U_SQ + 3 * D_SB
D_U = U_SG + D_SB


def _nt_dot(a, b):
    return lax.dot_general(a, b, (((1,), (1,)), ((), ())), preferred_element_type=F32)


def _tn_dot(a, b):
    return lax.dot_general(a, b, (((0,), (0,)), ((), ())), preferred_element_type=F32)


def _dot(a, b):
    return jnp.dot(a, b, preferred_element_type=F32)


def _split_bf16(x):
    hi = x.astype(BF16)
    lo = (x - hi.astype(F32)).astype(BF16)
    return hi, lo


def _neg_abs(x):
    bits = lax.bitcast_convert_type(x, jnp.uint32) | jnp.uint32(0x80000000)
    return lax.bitcast_convert_type(bits, F32)


def _sigmoid(x):
    return 1.0 / (1.0 + jnp.exp(-x))


def _log1pexp_neg_abs(z):
    return jnp.log(1.0 + jnp.exp(-jnp.abs(z)))


def _in_proj_kernel(x_ref, g_ref, wg_ref, ws_ref, wlr_ref, wa2_ref, ba_ref, u_ref, la_ref, h_scr,
                    *, gla_blocks):
    j = pl.program_id(1)

    @pl.when(j == 0)
    def _():
        x = x_ref[...]
        ms = jnp.mean(x * x, axis=-1, keepdims=True)
        hb = (x * lax.rsqrt(ms + EPS) * g_ref[...]).astype(BF16)
        h_scr[...] = hb
        g_lr = _dot(hb, wlr_ref[...])
        z = _dot(g_lr.astype(BF16), wa2_ref[...]) + ba_ref[...]
        la_ref[...] = (jnp.minimum(z, 0.0) - _log1pexp_neg_abs(z)) * (LOG2_E / GLA_TAU)

    @pl.when(j < gla_blocks)
    def _():
        u_ref[...] = _dot(h_scr[...], wg_ref[...]).astype(BF16)

    @pl.when(j >= gla_blocks)
    def _():
        u_ref[...] = _dot(h_scr[...], ws_ref[...]).astype(BF16)


def _in_proj(x2, g_pre, w_gla, w_sb, w_lr, w_a2, b_a, *, tm, tn):
    n, d = x2.shape
    gla_blocks = w_gla.shape[1] // tn
    du = w_gla.shape[1] + w_sb.shape[1]
    return pl.pallas_call(
        functools.partial(_in_proj_kernel, gla_blocks=gla_blocks),
        grid=(n // tm, du // tn),
        in_specs=[
            pl.BlockSpec((tm, d), lambda i, j: (i, 0)),
            pl.BlockSpec((1, d), lambda i, j: (0, 0)),
            pl.BlockSpec((d, tn), lambda i, j: (0, jnp.minimum(j, gla_blocks - 1))),
            pl.BlockSpec((d, tn), lambda i, j: (0, jnp.maximum(j - gla_blocks, 0))),
            pl.BlockSpec((d, LANES), lambda i, j: (0, 0)),
            pl.BlockSpec((LANES, D_QK), lambda i, j: (0, 0)),
            pl.BlockSpec((1, D_QK), lambda i, j: (0, 0)),
        ],
        out_specs=[
            pl.BlockSpec((tm, tn), lambda i, j: (i, j)),
            pl.BlockSpec((tm, D_QK), lambda i, j: (i, 0)),
        ],
        out_shape=[
            jax.ShapeDtypeStruct((n, du), BF16),
            jax.ShapeDtypeStruct((n, D_QK), F32),
        ],
        scratch_shapes=[pltpu.VMEM((tm, d), BF16)],
        compiler_params=pltpu.CompilerParams(
            dimension_semantics=("parallel", "arbitrary"),
            vmem_limit_bytes=VMEM_LIMIT_BYTES),
        name="in_proj",
    )(x2, g_pre, w_gla, w_sb, w_lr, w_a2, b_a)


def _gla_decay_matrix():
    c = GLA_CHUNK
    r = np.arange(c)[:, None]
    t = np.arange(c)[None, :]
    groups = []
    for lvl in range(GLA_LEVELS):
        s = 1 << lvl
        start = (r // (2 * s)) * (2 * s)
        m = start + s - 1
        upper = (r >= start + s) & (t > m) & (t <= r)
        lower = (r <= m) & (t > r) & (t <= m)
        groups.append(upper | lower)
    groups += [t <= r, t > r]
    a = np.concatenate(groups, axis=0).astype(np.float32)
    return np.concatenate([a, a], axis=1)


def _gla_level_matrix():
    c = GLA_CHUNK
    i = np.arange(c)[:, None]
    j = np.arange(c)[None, :]
    lvl = np.full((c, c), GLA_LEVELS + 1, np.int32)
    lvl[i == j] = GLA_LEVELS
    x = i ^ j
    for b in range(GLA_LEVELS):
        lvl[(i > j) & ((x >> b) == 1)] = b
    return lvl


def _gla_kernel(q_ref, k_ref, v_ref, gate_ref, la_ref, a_ref, lvl_ref, gh_ref, o_ref, st_ref,
                *, chunks_per_block):
    c = GLA_CHUNK

    @pl.when(pl.program_id(1) == 0)
    def _():
        st_ref[...] = jnp.zeros_like(st_ref)

    a_mat = a_ref[...]
    lvl = lvl_ref[...]
    g_head = gh_ref[...]
    scale = GLA_DK ** -0.5

    def chunk(ci, carry):
        r0 = pl.multiple_of(ci * c, c)
        rows = pl.ds(r0, c)
        la_parts = jnp.concatenate(_split_bf16(la_ref[rows, :]), axis=0)
        decay = jnp.exp2(_dot(a_mat, la_parts))
        for h in range(GLA_HEADS):
            kcols = slice(h * GLA_DK, (h + 1) * GLA_DK)
            vcols = slice(h * GLA_DV, (h + 1) * GLA_DV)
            q = q_ref[rows, kcols].astype(F32) * scale
            k = k_ref[rows, kcols].astype(F32)
            v = v_ref[rows, vcols]

            def dec(g):
                return decay[g * c:(g + 1) * c, kcols]

            scores = jnp.where(lvl == GLA_LEVELS, _nt_dot(q.astype(BF16), k.astype(BF16)), 0.0)
            for l in range(GLA_LEVELS):
                ql = (q * dec(l)).astype(BF16)
                kl = (k * dec(l)).astype(BF16)
                scores = jnp.where(lvl == l, _nt_dot(ql, kl), scores)
            intra = _dot(scores.astype(BF16), v)

            d_b = dec(GLA_LEVELS)
            d_last = dec(GLA_LEVELS + 1)
            st = st_ref[h]
            inter = _nt_dot((q * d_b).astype(BF16), st.astype(BF16))
            o = inter + intra

            kd = (k * d_last).astype(BF16)
            st_ref[h] = st * d_b[c - 1:c, :] + _tn_dot(v, kd)

            ms = jnp.mean(o * o, axis=-1, keepdims=True)
            gate = gate_ref[rows, vcols].astype(F32)
            o = o * lax.rsqrt(ms + EPS) * g_head * (gate * _sigmoid(gate))
            o_ref[rows, vcols] = o.astype(BF16)
        return carry

    lax.fori_loop(0, chunks_per_block, chunk, 0, unroll=4)


def _gla(u, log_a, g_head, *, batch, tc):
    n = u.shape[0]
    seq = n // batch
    nblk = seq // tc
    a_mat = jnp.asarray(_gla_decay_matrix(), BF16)
    lvl = jnp.asarray(_gla_level_matrix())
    rows = lambda b, i: b * nblk + i
    return pl.pallas_call(
        functools.partial(_gla_kernel, chunks_per_block=tc // GLA_CHUNK),
        grid=(batch, nblk),
        in_specs=[
            pl.BlockSpec((tc, D_QK), lambda b, i: (rows(b, i), U_GQ // D_QK)),
            pl.BlockSpec((tc, D_QK), lambda b, i: (rows(b, i), U_GK // D_QK)),
            pl.BlockSpec((tc, D_GLA), lambda b, i: (rows(b, i), U_GV // D_GLA)),
            pl.BlockSpec((tc, D_GLA), lambda b, i: (rows(b, i), U_GG // D_GLA)),
            pl.BlockSpec((tc, D_QK), lambda b, i: (rows(b, i), 0)),
            pl.BlockSpec(a_mat.shape, lambda b, i: (0, 0)),
            pl.BlockSpec(lvl.shape, lambda b, i: (0, 0)),
            pl.BlockSpec((1, GLA_DV), lambda b, i: (0, 0)),
        ],
        out_specs=pl.BlockSpec((tc, D_GLA), lambda b, i: (rows(b, i), 0)),
        out_shape=jax.ShapeDtypeStruct((n, D_GLA), BF16),
        scratch_shapes=[pltpu.VMEM((GLA_HEADS, GLA_DV, GLA_DK), F32)],
        compiler_params=pltpu.CompilerParams(
            dimension_semantics=("parallel", "arbitrary"),
            vmem_limit_bytes=VMEM_LIMIT_BYTES),
        name="gla",
    )(u, u, u, u, log_a, a_mat, lvl, g_head)


def _sb_kernel(q_ref, k_ref, v_ref, gate_ref, tri_ref, o_ref, carry_ref, acc_ref, *, tq, heads):
    qi = pl.program_id(2)
    tri = tri_ref[...]
    row = lax.broadcasted_iota(jnp.int32, (tq, tq), 0)
    col = lax.broadcasted_iota(jnp.int32, (tq, tq), 1)
    before = col < row
    hcols = [slice(h * SB_DH, (h + 1) * SB_DH) for h in range(heads)]
    qn = [(q_ref[:, c].astype(F32) * (-(SB_DH ** -0.5) * LOG2_E)).astype(BF16) for c in hcols]

    def block(h, kb, carry, diagonal):
        ks = pl.ds(pl.multiple_of(kb * tq, tq), tq)
        y = _nt_dot(qn[h], k_ref[ks, hcols[h]])
        soft = jnp.log2(1.0 + jnp.exp2(_neg_abs(y)))
        log_keep = jnp.minimum(y, 0.0) - soft
        if diagonal:
            log_keep = jnp.where(before, log_keep, 0.0)
        parts = jnp.concatenate(_split_bf16(log_keep), axis=1)
        suffix = _dot(parts, tri)
        w = jnp.exp2((log_keep - y) + suffix + carry)
        if diagonal:
            w = jnp.where(before, w, 0.0)
        out = _dot(w.astype(BF16), v_ref[ks, hcols[h]])
        return carry + jnp.sum(log_keep, axis=-1, keepdims=True), out

    zero = jnp.zeros((tq, 1), F32)

    @pl.when(qi == 0)
    def _():
        for h in range(heads):
            _, acc_ref[:, hcols[h]] = block(h, 0, zero, True)
        carry_ref[...] = jnp.full(carry_ref.shape, SB_LOG2_ZERO, F32)

    @pl.when(qi > 0)
    def _():
        for h in range(heads):
            c1, a1 = block(h, qi, zero, True)
            c2, a2 = block(h, qi - 1, c1, False)
            carry_ref[h] = c2
            acc_ref[:, hcols[h]] = a1 + a2

    def more(state):
        kb, live = state
        return jnp.logical_and(kb >= 0, live)

    def body(state):
        kb, _ = state
        top = None
        for h in range(heads):
            c, a = block(h, kb, carry_ref[h], False)
            carry_ref[h] = c
            acc_ref[:, hcols[h]] += a
            top = jnp.max(c) if top is None else jnp.maximum(top, jnp.max(c))
        return kb - 1, top > SB_LOG2_ZERO

    lax.while_loop(more, body, (qi - 2, jnp.max(carry_ref[...]) > SB_LOG2_ZERO))
    gate = gate_ref[...].astype(F32)
    o_ref[...] = (acc_ref[...] * (gate * _sigmoid(gate))).astype(BF16)


def _stick_breaking(u, *, batch, tq, heads):
    n = u.shape[0]
    seq = n // batch
    nq = seq // tq
    j = np.arange(tq)[:, None]
    s = np.arange(tq)[None, :]
    tri = jnp.asarray(np.tile((j > s).astype(np.float32), (2, 1)), BF16)
    width = heads * SB_DH
    rows = lambda b, g, i: b * nq + i
    return pl.pallas_call(
        functools.partial(_sb_kernel, tq=tq, heads=heads),
        grid=(batch, SB_HEADS // heads, nq),
        in_specs=[
            pl.BlockSpec((tq, width), lambda b, g, i: (rows(b, g, i), U_SQ // width + g)),
            pl.BlockSpec((seq, width), lambda b, g, i: (b, U_SK // width + g)),
            pl.BlockSpec((seq, width), lambda b, g, i: (b, U_SV // width + g)),
            pl.BlockSpec((tq, width), lambda b, g, i: (rows(b, g, i), U_SG // width + g)),
            pl.BlockSpec((2 * tq, tq), lambda b, g, i: (0, 0)),
        ],
        out_specs=pl.BlockSpec((tq, width), lambda b, g, i: (rows(b, g, i), g)),
        out_shape=jax.ShapeDtypeStruct((n, D_SB), BF16),
        scratch_shapes=[pltpu.VMEM((heads, tq, 1), F32), pltpu.VMEM((tq, width), F32)],
        compiler_params=pltpu.CompilerParams(
            dimension_semantics=("parallel", "parallel", "arbitrary"),
            vmem_limit_bytes=VMEM_LIMIT_BYTES),
        name="stick_breaking",
    )(u, u, u, u, tri)


def _out_kernel(og_ref, os_ref, x_ref, p_ref, wo_ref, gpost_ref, wg_ref, bg_ref, wp_ref, o_ref):
    mix = _dot(og_ref[...], wo_ref[:D_GLA, :]) + _dot(os_ref[...], wo_ref[D_GLA:, :])
    ms = jnp.mean(mix * mix, axis=-1, keepdims=True)
    h1 = x_ref[...] + mix * lax.rsqrt(ms + EPS) * gpost_ref[...]
    gate = _sigmoid(_dot(h1.astype(BF16), wg_ref[...]) + bg_ref[...])
    emb = _dot(p_ref[...].astype(BF16), wp_ref[...])
    o_ref[...] = h1 + gate * emb


def _out_proj(o_gla, o_sb, x2, p2, w_out, g_post, w_gate, b_gate, w_proj, *, tm):
    n, d = x2.shape
    dp = p2.shape[1]
    dmix = w_out.shape[0]
    const = lambda i: (0, 0)
    return pl.pallas_call(
        _out_kernel,
        grid=(n // tm,),
        in_specs=[
            pl.BlockSpec((tm, D_GLA), lambda i: (i, 0)),
            pl.BlockSpec((tm, D_SB), lambda i: (i, 0)),
            pl.BlockSpec((tm, d), lambda i: (i, 0)),
            pl.BlockSpec((tm, dp), lambda i: (i, 0)),
            pl.BlockSpec((dmix, d), const, pipeline_mode=pl.Buffered(1)),
            pl.BlockSpec((1, d), const),
            pl.BlockSpec((d, d), const, pipeline_mode=pl.Buffered(1)),
            pl.BlockSpec((1, d), const),
            pl.BlockSpec((dp, d), const, pipeline_mode=pl.Buffered(1)),
        ],
        out_specs=pl.BlockSpec((tm, d), lambda i: (i, 0)),
        out_shape=jax.ShapeDtypeStruct((n, d), F32),
        compiler_params=pltpu.CompilerParams(
            dimension_semantics=("parallel",),
            vmem_limit_bytes=VMEM_LIMIT_BYTES),
        name="out_proj",
    )(o_gla, o_sb, x2, p2, w_out, g_post, w_gate, b_gate, w_proj)


def kernel(x, p, g_pre, w_in, w_a2, b_a, g_gla_head, w_out, g_post, w_ple_gate, b_ple_gate, w_ple_proj):
    batch, seq, d = x.shape
    n = batch * seq
    depth = w_in.shape[0]
    lr0 = U_SQ
    h_res = x.astype(F32).reshape(n, d)
    for i in range(depth):
        w = w_in[i]
        w_gla = w[:, :lr0].astype(BF16)
        w_sb = w[:, lr0 + GLA_RANK:].astype(BF16)
        w_lr = jnp.pad(w[:, lr0:lr0 + GLA_RANK], ((0, 0), (0, LANES - GLA_RANK))).astype(BF16)
        w_a2p = jnp.pad(w_a2[i], ((0, LANES - GLA_RANK), (0, 0))).astype(BF16)
        u, log_a = _in_proj(h_res, g_pre[i][None, :], w_gla, w_sb, w_lr, w_a2p, b_a[i][None, :],
                            tm=1024, tn=1024)
        o_gla = _gla(u, log_a, g_gla_head[i][None, :], batch=batch, tc=512)
        o_sb = _stick_breaking(u, batch=batch, tq=256, heads=8)
        h_res = _out_proj(o_gla, o_sb, h_res, p[i].astype(F32).reshape(n, -1),
                          w_out[i].astype(BF16), g_post[i][None, :],
                          w_ple_gate[i].astype(BF16), b_ple_gate[i][None, :],
                          w_ple_proj[i].astype(BF16), tm=512)
    return h_res.reshape(batch, seq, d).astype(x.dtype)
```

```python
import functools

import numpy as np
import jax
import jax.numpy as jnp
from jax import lax
from jax.experimental import pallas as pl
from jax.experimental.pallas import tpu as pltpu

F32 = jnp.float32
BF16 = jnp.bfloat16

EPS = 1e-6
GLA_HEADS = 4
GLA_DK = 128
GLA_DV = 256
GLA_RANK = 16
GLA_TAU = 16.0
GLA_CHUNK = 64
GLA_LEVELS = 6
GLA_GROUP = 4
SB_HEADS = 8
SB_DH = 128

LOG2_E = 1.4426950408889634
SB_LOG2_ZERO = -160.0
SB_SKEW = 1

LANES = 128
VMEM_LIMIT_BYTES = 56 * 1024 * 1024

D_QK = GLA_HEADS * GLA_DK
D_GLA = GLA_HEADS * GLA_DV
D_SB = SB_HEADS * SB_DH
U_GQ, U_GK, U_GV, U_GG = 0, D_QK, 2 * D_QK, 2 * D_QK + D_GLA
U_SQ = U_GG + D_GLA
U_SK, U_SV, U_SG = U_SQ + D_SB, U_SQ + 2 * D_SB, U_SQ + 3 * D_SB
D_U = U_SG + D_SB


def _nt_dot(a, b):
    return lax.dot_general(a, b, (((1,), (1,)), ((), ())), preferred_element_type=F32)


def _tn_dot(a, b):
    return lax.dot_general(a, b, (((0,), (0,)), ((), ())), preferred_element_type=F32)


def _dot(a, b):
    return jnp.dot(a, b, preferred_element_type=F32)


def _split_bf16(x):
    hi = x.astype(BF16)
    lo = (x - hi.astype(F32)).astype(BF16)
    return hi, lo


def _sigmoid(x):
    return 1.0 / (1.0 + jnp.exp(-x))


def _log1pexp_neg_abs(z):
    return jnp.log(1.0 + jnp.exp(-jnp.abs(z)))


def _in_proj_kernel(x_ref, g_ref, wg_ref, ws_ref, wlr_ref, wa2_ref, ba_ref, u_ref, la_ref, h_scr,
                    *, gla_blocks):
    j = pl.program_id(1)

    @pl.when(j == 0)
    def _():
        x = x_ref[...]
        ms = jnp.mean(x * x, axis=-1, keepdims=True)
        hb = (x * lax.rsqrt(ms + EPS) * g_ref[...]).astype(BF16)
        h_scr[...] = hb
        g_lr = _dot(hb, wlr_ref[...])
        z = _dot(g_lr.astype(BF16), wa2_ref[...]) + ba_ref[...]
        la_ref[...] = (jnp.minimum(z, 0.0) - _log1pexp_neg_abs(z)) * (LOG2_E / GLA_TAU)

    w = jnp.where(j < gla_blocks, wg_ref[...], ws_ref[...])
    u_ref[...] = _dot(h_scr[...], w).astype(BF16)


def _in_proj(x2, g_pre, w_gla, w_sb, w_lr, w_a2, b_a, *, tm, tn):
    n, d = x2.shape
    gla_blocks = w_gla.shape[1] // tn
    du = w_gla.shape[1] + w_sb.shape[1]
    return pl.pallas_call(
        functools.partial(_in_proj_kernel, gla_blocks=gla_blocks),
        grid=(n // tm, du // tn),
        in_specs=[
            pl.BlockSpec((tm, d), lambda i, j: (i, 0)),
            pl.BlockSpec((1, d), lambda i, j: (0, 0)),
            pl.BlockSpec((d, tn), lambda i, j: (0, jnp.minimum(j, gla_blocks - 1))),
            pl.BlockSpec((d, tn), lambda i, j: (0, jnp.maximum(j - gla_blocks, 0))),
            pl.BlockSpec((d, LANES), lambda i, j: (0, 0)),
            pl.BlockSpec((LANES, D_QK), lambda i, j: (0, 0)),
            pl.BlockSpec((1, D_QK), lambda i, j: (0, 0)),
        ],
        out_specs=[
            pl.BlockSpec((tm, tn), lambda i, j: (i, j)),
            pl.BlockSpec((tm, D_QK), lambda i, j: (i, 0)),
        ],
        out_shape=[
            jax.ShapeDtypeStruct((n, du), BF16),
            jax.ShapeDtypeStruct((n, D_QK), F32),
        ],
        scratch_shapes=[pltpu.VMEM((tm, d), BF16)],
        compiler_params=pltpu.CompilerParams(
            dimension_semantics=("parallel", "arbitrary"),
            vmem_limit_bytes=VMEM_LIMIT_BYTES),
        name="in_proj",
    )(x2, g_pre, w_gla, w_sb, w_lr, w_a2, b_a)


def _gla_decay_matrix():
    c = GLA_CHUNK
    r = np.arange(c)[:, None]
    t = np.arange(c)[None, :]
    groups = []
    for lvl in range(GLA_LEVELS):
        s = 1 << lvl
        start = (r // (2 * s)) * (2 * s)
        m = start + s - 1
        upper = (r >= start + s) & (t > m) & (t <= r)
        lower = (r <= m) & (t > r) & (t <= m)
        groups.append(upper | lower)
    groups += [t <= r, t > r]
    a = np.concatenate(groups, axis=0).astype(np.float32)
    return np.concatenate([a, a], axis=1)


def _gla_level_matrix():
    c = GLA_CHUNK
    i = np.arange(c)[:, None]
    j = np.arange(c)[None, :]
    lvl = np.full((c, c), GLA_LEVELS + 1, np.int32)
    lvl[i == j] = GLA_LEVELS
    x = i ^ j
    for b in range(GLA_LEVELS):
        lvl[(i > j) & ((x >> b) == 1)] = b
    return lvl


def _gla_kernel(q_ref, k_ref, v_ref, gate_ref, la_ref, a_ref, lvl_ref, gh_ref, o_ref, st_ref,
                *, chunks_per_block):
    c = GLA_CHUNK

    @pl.when(pl.program_id(1) == 0)
    def _():
        st_ref[...] = jnp.zeros_like(st_ref)

    a_mat = a_ref[...]
    lvl = lvl_ref[...]
    g_head = gh_ref[...]
    scale = GLA_DK ** -0.5

    def group(gi, carry):
        rows = [pl.ds(pl.multiple_of((gi * GLA_GROUP + j) * c, c), c) for j in range(GLA_GROUP)]
        kcols = [slice(h * GLA_DK, (h + 1) * GLA_DK) for h in range(GLA_HEADS)]
        vcols = [slice(h * GLA_DV, (h + 1) * GLA_DV) for h in range(GLA_HEADS)]
        pairs = [(j, h) for j in range(GLA_GROUP) for h in range(GLA_HEADS)]

        decay = []
        for j in range(GLA_GROUP):
            la_parts = jnp.concatenate(_split_bf16(la_ref[rows[j], :]), axis=0)
            decay.append(jnp.exp2(_dot(a_mat, la_parts)))

        def dec(j, h, g):
            return decay[j][g * c:(g + 1) * c, kcols[h]]

        q = {(j, h): q_ref[rows[j], kcols[h]].astype(F32) * scale for j, h in pairs}
        k = {(j, h): k_ref[rows[j], kcols[h]].astype(F32) for j, h in pairs}

        scores = {p: jnp.where(lvl == GLA_LEVELS, _nt_dot(q[p].astype(BF16), k[p].astype(BF16)), 0.0)
                  for p in pairs}
        for l in range(GLA_LEVELS):
            for j, h in pairs:
                ql = (q[j, h] * dec(j, h, l)).astype(BF16)
                kl = (k[j, h] * dec(j, h, l)).astype(BF16)
                scores[j, h] = jnp.where(lvl == l, _nt_dot(ql, kl), scores[j, h])

        intra = {(j, h): _dot(scores[j, h].astype(BF16), v_ref[rows[j], vcols[h]]) for j, h in pairs}

        out = {}
        for j in range(GLA_GROUP):
            st = [st_ref[h] for h in range(GLA_HEADS)]
            for h in range(GLA_HEADS):
                d_b = dec(j, h, GLA_LEVELS)
                out[j, h] = intra[j, h] + _nt_dot((q[j, h] * d_b).astype(BF16), st[h].astype(BF16))
            for h in range(GLA_HEADS):
                d_b = dec(j, h, GLA_LEVELS)
                kd = (k[j, h] * dec(j, h, GLA_LEVELS + 1)).astype(BF16)
                st_ref[h] = st[h] * d_b[c - 1:c, :] + _tn_dot(v_ref[rows[j], vcols[h]], kd)

        for j, h in pairs:
            o = out[j, h]
            ms = jnp.mean(o * o, axis=-1, keepdims=True)
            gate = gate_ref[rows[j], vcols[h]].astype(F32)
            o = o * lax.rsqrt(ms + EPS) * g_head * (gate * _sigmoid(gate))
            o_ref[rows[j], vcols[h]] = o.astype(BF16)
        return carry

    lax.fori_loop(0, chunks_per_block // GLA_GROUP, group, 0)


def _gla(u, log_a, g_head, *, batch, tc):
    n = u.shape[0]
    seq = n // batch
    nblk = seq // tc
    assert seq % tc == 0 and tc % (GLA_CHUNK * GLA_GROUP) == 0, (seq, tc)
    a_mat = jnp.asarray(_gla_decay_matrix(), BF16)
    lvl = jnp.asarray(_gla_level_matrix())
    rows = lambda b, i: b * nblk + i
    return pl.pallas_call(
        functools.partial(_gla_kernel, chunks_per_block=tc // GLA_CHUNK),
        grid=(batch, nblk),
        in_specs=[
            pl.BlockSpec((tc, D_QK), lambda b, i: (rows(b, i), U_GQ // D_QK)),
            pl.BlockSpec((tc, D_QK), lambda b, i: (rows(b, i), U_GK // D_QK)),
            pl.BlockSpec((tc, D_GLA), lambda b, i: (rows(b, i), U_GV // D_GLA)),
            pl.BlockSpec((tc, D_GLA), lambda b, i: (rows(b, i), U_GG // D_GLA)),
            pl.BlockSpec((tc, D_QK), lambda b, i: (rows(b, i), 0)),
            pl.BlockSpec(a_mat.shape, lambda b, i: (0, 0)),
            pl.BlockSpec(lvl.shape, lambda b, i: (0, 0)),
            pl.BlockSpec((1, GLA_DV), lambda b, i: (0, 0)),
        ],
        out_specs=pl.BlockSpec((tc, D_GLA), lambda b, i: (rows(b, i), 0)),
        out_shape=jax.ShapeDtypeStruct((n, D_GLA), BF16),
        scratch_shapes=[pltpu.VMEM((GLA_HEADS, GLA_DV, GLA_DK), F32)],
        compiler_params=pltpu.CompilerParams(
            dimension_semantics=("parallel", "arbitrary"),
            vmem_limit_bytes=VMEM_LIMIT_BYTES),
        name="gla",
    )(u, u, u, u, log_a, a_mat, lvl, g_head)


def _sb_kernel(q_ref, k_ref, v_ref, gate_ref, tri_ref, o_ref, carry_ref, acc_ref, *, tq, heads):
    qi = pl.program_id(2)
    tri = tri_ref[...]
    row = lax.broadcasted_iota(jnp.int32, (tq, tq), 0)
    col = lax.broadcasted_iota(jnp.int32, (tq, tq), 1)
    before = col < row
    hcols = [slice(h * SB_DH, (h + 1) * SB_DH) for h in range(heads)]
    qn = [(q_ref[:, c].astype(F32) * (-(SB_DH ** -0.5) * LOG2_E)).astype(BF16) for c in hcols]

    def logits_stage(h, kb, diagonal):
        ks = pl.ds(pl.multiple_of(kb * tq, tq), tq)
        y = _nt_dot(qn[h], k_ref[ks, hcols[h]])
        soft = jnp.log2(1.0 + jnp.exp2(-jnp.abs(y)))
        log_keep = jnp.minimum(y, 0.0) - soft
        if diagonal:
            log_keep = jnp.where(before, log_keep, 0.0)
        return log_keep - y, log_keep.astype(BF16), jnp.sum(log_keep, axis=-1, keepdims=True)

    def weights_stage(log_beta, keep_bf16, carry, diagonal):
        suffix = _dot(keep_bf16, tri)
        w = jnp.exp2(log_beta + suffix + carry)
        if diagonal:
            w = jnp.where(before, w, 0.0)
        return w.astype(BF16)

    def output_stage(h, kb, w):
        ks = pl.ds(pl.multiple_of(kb * tq, tq), tq)
        return _dot(w, v_ref[ks, hcols[h]])

    def run_blocks(blocks):
        n = len(blocks)
        stage_a, stage_b, carries, outs = [None] * n, [None] * n, [None] * n, [None] * n
        for t in range(n + 2 * SB_SKEW):
            if t < n:
                h, kb, diagonal, _, _ = blocks[t]
                stage_a[t] = logits_stage(h, kb, diagonal)
            b = t - SB_SKEW
            if 0 <= b < n:
                h, kb, diagonal, prev, carry = blocks[b]
                if prev is not None:
                    carry = carries[prev]
                log_beta, keep_bf16, row_sum = stage_a[b]
                stage_b[b] = weights_stage(log_beta, keep_bf16, carry, diagonal)
                carries[b] = carry + row_sum
            c = t - 2 * SB_SKEW
            if 0 <= c < n:
                h, kb, _, _, _ = blocks[c]
                outs[c] = output_stage(h, kb, stage_b[c])
        return carries, outs

    zero = jnp.zeros((tq, 1), F32)

    @pl.when(qi == 0)
    def _():
        _, outs = run_blocks([(h, 0, True, None, zero) for h in range(heads)])
        for h in range(heads):
            acc_ref[:, hcols[h]] = outs[h]
        carry_ref[...] = jnp.full(carry_ref.shape, SB_LOG2_ZERO, F32)

    @pl.when(qi > 0)
    def _():
        blocks = []
        for h in range(heads):
            blocks += [(h, qi, True, None, zero), (h, qi - 1, False, 2 * h, None)]
        carries, outs = run_blocks(blocks)
        for h in range(heads):
            carry_ref[h] = carries[2 * h + 1]
            acc_ref[:, hcols[h]] = outs[2 * h] + outs[2 * h + 1]

    def more(state):
        kb, live = state
        return jnp.logical_and(kb >= 0, live)

    def body(state):
        kb, _ = state
        carries, outs = run_blocks([(h, kb, False, None, carry_ref[h]) for h in range(heads)])
        top = None
        for h in range(heads):
            carry_ref[h] = carries[h]
            acc_ref[:, hcols[h]] += outs[h]
            top = jnp.max(carries[h]) if top is None else jnp.maximum(top, jnp.max(carries[h]))
        return kb - 1, top > SB_LOG2_ZERO

    lax.while_loop(more, body, (qi - 2, jnp.max(carry_ref[...]) > SB_LOG2_ZERO))
    gate = gate_ref[...].astype(F32)
    o_ref[...] = (acc_ref[...] * (gate * _sigmoid(gate))).astype(BF16)


def _stick_breaking(u, *, batch, tq, heads):
    n = u.shape[0]
    seq = n // batch
    nq = seq // tq
    j = np.arange(tq)[:, None]
    s = np.arange(tq)[None, :]
    tri = jnp.asarray((j > s).astype(np.float32), BF16)
    width = heads * SB_DH
    rows = lambda b, g, i: b * nq + i
    return pl.pallas_call(
        functools.partial(_sb_kernel, tq=tq, heads=heads),
        grid=(batch, SB_HEADS // heads, nq),
        in_specs=[
            pl.BlockSpec((tq, width), lambda b, g, i: (rows(b, g, i), U_SQ // width + g)),
            pl.BlockSpec((seq, width), lambda b, g, i: (b, U_SK // width + g)),
            pl.BlockSpec((seq, width), lambda b, g, i: (b, U_SV // width + g)),
            pl.BlockSpec((tq, width), lambda b, g, i: (rows(b, g, i), U_SG // width + g)),
            pl.BlockSpec((tq, tq), lambda b, g, i: (0, 0)),
        ],
        out_specs=pl.BlockSpec((tq, width), lambda b, g, i: (rows(b, g, i), g)),
        out_shape=jax.ShapeDtypeStruct((n, D_SB), BF16),
        scratch_shapes=[pltpu.VMEM((heads, tq, 1), F32), pltpu.VMEM((tq, width), F32)],
        compiler_params=pltpu.CompilerParams(
            dimension_semantics=("parallel", "parallel", "arbitrary"),
            vmem_limit_bytes=VMEM_LIMIT_BYTES),
        name="stick_breaking",
    )(u, u, u, u, tri)


def _out_kernel(og_ref, os_ref, x_ref, p_ref, wo_ref, gpost_ref, wg_ref, bg_ref, wp_ref, o_ref):
    mix = _dot(og_ref[...], wo_ref[:D_GLA, :]) + _dot(os_ref[...], wo_ref[D_GLA:, :])
    ms = jnp.mean(mix * mix, axis=-1, keepdims=True)
    h1 = x_ref[...] + mix * lax.rsqrt(ms + EPS) * gpost_ref[...]
    gate = _sigmoid(_dot(h1.astype(BF16), wg_ref[...]) + bg_ref[...])
    emb = _dot(p_ref[...].astype(BF16), wp_ref[...])
    o_ref[...] = h1 + gate * emb


def _out_proj(o_gla, o_sb, x2, p2, w_out, g_post, w_gate, b_gate, w_proj, *, tm):
    n, d = x2.shape
    dp = p2.shape[1]
    dmix = w_out.shape[0]
    const = lambda i: (0, 0)
    return pl.pallas_call(
        _out_kernel,
        grid=(n // tm,),
        in_specs=[
            pl.BlockSpec((tm, D_GLA), lambda i: (i, 0)),
            pl.BlockSpec((tm, D_SB), lambda i: (i, 0)),
            pl.BlockSpec((tm, d), lambda i: (i, 0)),
            pl.BlockSpec((tm, dp), lambda i: (i, 0)),
            pl.BlockSpec((dmix, d), const, pipeline_mode=pl.Buffered(1)),
            pl.BlockSpec((1, d), const),
            pl.BlockSpec((d, d), const, pipeline_mode=pl.Buffered(1)),
            pl.BlockSpec((1, d), const),
            pl.BlockSpec((dp, d), const, pipeline_mode=pl.Buffered(1)),
        ],
        out_specs=pl.BlockSpec((tm, d), lambda i: (i, 0)),
        out_shape=jax.ShapeDtypeStruct((n, d), F32),
        compiler_params=pltpu.CompilerParams(
            dimension_semantics=("parallel",),
            vmem_limit_bytes=VMEM_LIMIT_BYTES),
        name="out_proj",
    )(o_gla, o_sb, x2, p2, w_out, g_post, w_gate, b_gate, w_proj)


def kernel(x, p, g_pre, w_in, w_a2, b_a, g_gla_head, w_out, g_post, w_ple_gate, b_ple_gate, w_ple_proj):
    batch, seq, d = x.shape
    n = batch * seq
    depth = w_in.shape[0]
    lr0 = U_SQ
    h_res = x.astype(F32).reshape(n, d)
    for i in range(depth):
        w = w_in[i]
        w_gla = w[:, :lr0].astype(BF16)
        w_sb = w[:, lr0 + GLA_RANK:].astype(BF16)
        w_lr = jnp.pad(w[:, lr0:lr0 + GLA_RANK], ((0, 0), (0, LANES - GLA_RANK))).astype(BF16)
        w_a2p = jnp.pad(w_a2[i], ((0, LANES - GLA_RANK), (0, 0))).astype(BF16)
        u, log_a = _in_proj(h_res, g_pre[i][None, :], w_gla, w_sb, w_lr, w_a2p, b_a[i][None, :],
                            tm=1024, tn=1024)
        o_gla = _gla(u, log_a, g_gla_head[i][None, :], batch=batch, tc=512)
        o_sb = _stick_breaking(u, batch=batch, tq=256, heads=8)
        h_res = _out_proj(o_gla, o_sb, h_res, p[i].astype(F32).reshape(n, -1),
                          w_out[i].astype(BF16), g_post[i][None, :],
                          w_ple_gate[i].astype(BF16), b_ple_gate[i][None, :],
                          w_ple_proj[i].astype(BF16), tm=512)
    return h_res.reshape(batch, seq, d).astype(x.dtype)
```

```python
import functools

import numpy as np
import jax
import jax.numpy as jnp
from jax import lax
from jax.experimental import pallas as pl
from jax.experimental.pallas import tpu as pltpu

F32 = jnp.float32
BF16 = jnp.bfloat16

EPS = 1e-6
GLA_HEADS = 4
GLA_DK = 128
GLA_DV = 256
GLA_RANK = 16
GLA_TAU = 16.0
GLA_CHUNK = 64
GLA_LEVELS = 6
GLA_GROUP = 4
SB_HEADS = 8
SB_DH = 128

LOG2_E = 1.4426950408889634
SB_LOG2_ZERO = -160.0
SB_SKEW = 1

LANES = 128
VMEM_LIMIT_BYTES = 56 * 1024 * 1024

D_QK = GLA_HEADS * GLA_DK
D_GLA = GLA_HEADS * GLA_DV
D_SB = SB_HEADS * SB_DH
U_GQ, U_GK, U_GV, U_GG = 0, D_QK, 2 * D_QK, 2 * D_QK + D_GLA
U_SQ = U_GG + D_GLA
U_SK, U_SV, U_SG = U_SQ + D_SB, U_SQ + 2 * D_SB, U_SQ + 3 * D_SB
D_U = U_SG + D_SB


def _nt_dot(a, b):
    return lax.dot_general(a, b, (((1,), (1,)), ((), ())), preferred_element_type=F32)


def _tn_dot(a, b):
    return lax.dot_general(a, b, (((0,), (0,)), ((), ())), preferred_element_type=F32)


def _dot(a, b):
    return jnp.dot(a, b, preferred_element_type=F32)


def _split_bf16(x):
    hi = x.astype(BF16)
    lo = (x - hi.astype(F32)).astype(BF16)
    return hi, lo


def _sigmoid(x):
    return 1.0 / (1.0 + jnp.exp(-x))


def _log1pexp_neg_abs(z):
    return jnp.log(1.0 + jnp.exp(-jnp.abs(z)))


def _w_prep_kernel(w_ref, o_ref):
    o_ref[:, :U_SQ] = w_ref[:, :U_SQ].astype(BF16)
    o_ref[:, U_SQ:] = w_ref[:, U_SQ + GLA_RANK:].astype(BF16)


def _w_prep(w, *, tr):
    d, d_in = w.shape
    return pl.pallas_call(
        _w_prep_kernel,
        grid=(d // tr,),
        in_specs=[pl.BlockSpec((tr, d_in), lambda r: (r, 0))],
        out_specs=pl.BlockSpec((tr, D_U), lambda r: (r, 0)),
        out_shape=jax.ShapeDtypeStruct((d, D_U), BF16),
        compiler_params=pltpu.CompilerParams(
            dimension_semantics=("parallel",),
            vmem_limit_bytes=VMEM_LIMIT_BYTES),
        name="w_prep",
    )(w)


def _in_proj_kernel(x_ref, g_ref, w_ref, wlr_ref, wa2_ref, ba_ref, u_ref, la_ref, h_scr):
    @pl.when(pl.program_id(1) == 0)
    def _():
        x = x_ref[...]
        ms = jnp.mean(x * x, axis=-1, keepdims=True)
        hb = (x * lax.rsqrt(ms + EPS) * g_ref[...]).astype(BF16)
        h_scr[...] = hb
        g_lr = _dot(hb, wlr_ref[...])
        z = _dot(g_lr.astype(BF16), wa2_ref[...]) + ba_ref[...]
        la_ref[...] = (jnp.minimum(z, 0.0) - _log1pexp_neg_abs(z)) * (LOG2_E / GLA_TAU)

    u_ref[...] = _dot(h_scr[...], w_ref[...]).astype(BF16)


def _in_proj(x2, g_pre, w_main, w_lr, w_a2, b_a, *, tm, tn):
    n, d = x2.shape
    du = w_main.shape[1]
    return pl.pallas_call(
        _in_proj_kernel,
        grid=(n // tm, du // tn),
        in_specs=[
            pl.BlockSpec((tm, d), lambda i, j: (i, 0)),
            pl.BlockSpec((1, d), lambda i, j: (0, 0)),
            pl.BlockSpec((d, tn), lambda i, j: (0, j)),
            pl.BlockSpec((d, LANES), lambda i, j: (0, 0)),
            pl.BlockSpec((LANES, D_QK), lambda i, j: (0, 0)),
            pl.BlockSpec((1, D_QK), lambda i, j: (0, 0)),
        ],
        out_specs=[
            pl.BlockSpec((tm, tn), lambda i, j: (i, j)),
            pl.BlockSpec((tm, D_QK), lambda i, j: (i, 0)),
        ],
        out_shape=[
            jax.ShapeDtypeStruct((n, du), BF16),
            jax.ShapeDtypeStruct((n, D_QK), F32),
        ],
        scratch_shapes=[pltpu.VMEM((tm, d), BF16)],
        compiler_params=pltpu.CompilerParams(
            dimension_semantics=("parallel", "arbitrary"),
            vmem_limit_bytes=VMEM_LIMIT_BYTES),
        name="in_proj",
    )(x2, g_pre, w_main, w_lr, w_a2, b_a)


def _gla_decay_matrix():
    c = GLA_CHUNK
    r = np.arange(c)[:, None]
    t = np.arange(c)[None, :]
    groups = []
    for lvl in range(GLA_LEVELS):
        s = 1 << lvl
        start = (r // (2 * s)) * (2 * s)
        m = start + s - 1
        upper = (r >= start + s) & (t > m) & (t <= r)
        lower = (r <= m) & (t > r) & (t <= m)
        groups.append(upper | lower)
    groups += [t <= r, t > r]
    a = np.concatenate(groups, axis=0).astype(np.float32)
    return np.concatenate([a, a], axis=1)


def _gla_level_matrix():
    c = GLA_CHUNK
    i = np.arange(c)[:, None]
    j = np.arange(c)[None, :]
    lvl = np.full((c, c), GLA_LEVELS + 1, np.int32)
    lvl[i == j] = GLA_LEVELS
    x = i ^ j
    for b in range(GLA_LEVELS):
        lvl[(i > j) & ((x >> b) == 1)] = b
    return lvl


def _gla_kernel(q_ref, k_ref, v_ref, gate_ref, la_ref, a_ref, lvl_ref, gh_ref, o_ref, st_ref,
                *, chunks_per_block):
    c = GLA_CHUNK

    @pl.when(pl.program_id(1) == 0)
    def _():
        st_ref[...] = jnp.zeros_like(st_ref)

    a_mat = a_ref[...]
    lvl = lvl_ref[...]
    g_head = gh_ref[...]
    scale = GLA_DK ** -0.5

    def group(gi, carry):
        rows = [pl.ds(pl.multiple_of((gi * GLA_GROUP + j) * c, c), c) for j in range(GLA_GROUP)]
        kcols = [slice(h * GLA_DK, (h + 1) * GLA_DK) for h in range(GLA_HEADS)]
        vcols = [slice(h * GLA_DV, (h + 1) * GLA_DV) for h in range(GLA_HEADS)]
        pairs = [(j, h) for j in range(GLA_GROUP) for h in range(GLA_HEADS)]

        decay = []
        for j in range(GLA_GROUP):
            la_parts = jnp.concatenate(_split_bf16(la_ref[rows[j], :]), axis=0)
            decay.append(jnp.exp2(_dot(a_mat, la_parts)))

        def dec(j, h, g):
            return decay[j][g * c:(g + 1) * c, kcols[h]]

        q = {(j, h): q_ref[rows[j], kcols[h]].astype(F32) * scale for j, h in pairs}
        k = {(j, h): k_ref[rows[j], kcols[h]].astype(F32) for j, h in pairs}

        scores = {p: jnp.where(lvl == GLA_LEVELS, _nt_dot(q[p].astype(BF16), k[p].astype(BF16)), 0.0)
                  for p in pairs}
        for l in range(GLA_LEVELS):
            for j, h in pairs:
                ql = (q[j, h] * dec(j, h, l)).astype(BF16)
                kl = (k[j, h] * dec(j, h, l)).astype(BF16)
                scores[j, h] = jnp.where(lvl == l, _nt_dot(ql, kl), scores[j, h])

        intra = {(j, h): _dot(scores[j, h].astype(BF16), v_ref[rows[j], vcols[h]]) for j, h in pairs}

        out = {}
        for j in range(GLA_GROUP):
            st = [st_ref[h] for h in range(GLA_HEADS)]
            for h in range(GLA_HEADS):
                d_b = dec(j, h, GLA_LEVELS)
                out[j, h] = intra[j, h] + _nt_dot((q[j, h] * d_b).astype(BF16), st[h].astype(BF16))
            for h in range(GLA_HEADS):
                d_b = dec(j, h, GLA_LEVELS)
                kd = (k[j, h] * dec(j, h, GLA_LEVELS + 1)).astype(BF16)
                st_ref[h] = st[h] * d_b[c - 1:c, :] + _tn_dot(v_ref[rows[j], vcols[h]], kd)

        for j, h in pairs:
            o = out[j, h]
            ms = jnp.mean(o * o, axis=-1, keepdims=True)
            gate = gate_ref[rows[j], vcols[h]].astype(F32)
            o = o * lax.rsqrt(ms + EPS) * g_head * (gate * _sigmoid(gate))
            o_ref[rows[j], vcols[h]] = o.astype(BF16)
        return carry

    lax.fori_loop(0, chunks_per_block // GLA_GROUP, group, 0)


def _gla(u, log_a, g_head, *, batch, tc):
    n = u.shape[0]
    seq = n // batch
    nblk = seq // tc
    assert seq % tc == 0 and tc % (GLA_CHUNK * GLA_GROUP) == 0, (seq, tc)
    a_mat = jnp.asarray(_gla_decay_matrix(), BF16)
    lvl = jnp.asarray(_gla_level_matrix())
    rows = lambda b, i: b * nblk + i
    return pl.pallas_call(
        functools.partial(_gla_kernel, chunks_per_block=tc // GLA_CHUNK),
        grid=(batch, nblk),
        in_specs=[
            pl.BlockSpec((tc, D_QK), lambda b, i: (rows(b, i), U_GQ // D_QK)),
            pl.BlockSpec((tc, D_QK), lambda b, i: (rows(b, i), U_GK // D_QK)),
            pl.BlockSpec((tc, D_GLA), lambda b, i: (rows(b, i), U_GV // D_GLA)),
            pl.BlockSpec((tc, D_GLA), lambda b, i: (rows(b, i), U_GG // D_GLA)),
            pl.BlockSpec((tc, D_QK), lambda b, i: (rows(b, i), 0)),
            pl.BlockSpec(a_mat.shape, lambda b, i: (0, 0)),
            pl.BlockSpec(lvl.shape, lambda b, i: (0, 0)),
            pl.BlockSpec((1, GLA_DV), lambda b, i: (0, 0)),
        ],
        out_specs=pl.BlockSpec((tc, D_GLA), lambda b, i: (rows(b, i), 0)),
        out_shape=jax.ShapeDtypeStruct((n, D_GLA), BF16),
        scratch_shapes=[pltpu.VMEM((GLA_HEADS, GLA_DV, GLA_DK), F32)],
        compiler_params=pltpu.CompilerParams(
            dimension_semantics=("parallel", "arbitrary"),
            vmem_limit_bytes=VMEM_LIMIT_BYTES),
        name="gla",
    )(u, u, u, u, log_a, a_mat, lvl, g_head)


def _sb_kernel(q_ref, k_ref, v_ref, gate_ref, tri_ref, o_ref, carry_ref, acc_ref, *, tq, heads):
    qi = pl.program_id(2)
    tri = tri_ref[...]
    row = lax.broadcasted_iota(jnp.int32, (tq, tq), 0)
    col = lax.broadcasted_iota(jnp.int32, (tq, tq), 1)
    before = col < row
    hcols = [slice(h * SB_DH, (h + 1) * SB_DH) for h in range(heads)]
    qn = [(q_ref[:, c].astype(F32) * (-(SB_DH ** -0.5) * LOG2_E)).astype(BF16) for c in hcols]

    def logits_stage(h, kb, diagonal):
        ks = pl.ds(pl.multiple_of(kb * tq, tq), tq)
        y = _nt_dot(qn[h], k_ref[ks, hcols[h]])
        soft = jnp.log2(1.0 + jnp.exp2(-jnp.abs(y)))
        log_keep = jnp.minimum(y, 0.0) - soft
        if diagonal:
            log_keep = jnp.where(before, log_keep, 0.0)
        return log_keep - y, log_keep.astype(BF16), jnp.sum(log_keep, axis=-1, keepdims=True)

    def weights_stage(log_beta, keep_bf16, carry, diagonal):
        suffix = _dot(keep_bf16, tri)
        w = jnp.exp2(log_beta + suffix + carry)
        if diagonal:
            w = jnp.where(before, w, 0.0)
        return w.astype(BF16)

    def output_stage(h, kb, w):
        ks = pl.ds(pl.multiple_of(kb * tq, tq), tq)
        return _dot(w, v_ref[ks, hcols[h]])

    def run_blocks(blocks):
        n = len(blocks)
        stage_a, stage_b, carries, outs = [None] * n, [None] * n, [None] * n, [None] * n
        for t in range(n + 2 * SB_SKEW):
            if t < n:
                h, kb, diagonal, _, _ = blocks[t]
                stage_a[t] = logits_stage(h, kb, diagonal)
            b = t - SB_SKEW
            if 0 <= b < n:
                h, kb, diagonal, prev, carry = blocks[b]
                if prev is not None:
                    carry = carries[prev]
                log_beta, keep_bf16, row_sum = stage_a[b]
                stage_b[b] = weights_stage(log_beta, keep_bf16, carry, diagonal)
                carries[b] = carry + row_sum
            c = t - 2 * SB_SKEW
            if 0 <= c < n:
                h, kb, _, _, _ = blocks[c]
                outs[c] = output_stage(h, kb, stage_b[c])
        return carries, outs

    zero = jnp.zeros((tq, 1), F32)

    @pl.when(qi == 0)
    def _():
        _, outs = run_blocks([(h, 0, True, None, zero) for h in range(heads)])
        for h in range(heads):
            acc_ref[:, hcols[h]] = outs[h]
        carry_ref[...] = jnp.full(carry_ref.shape, SB_LOG2_ZERO, F32)

    @pl.when(qi > 0)
    def _():
        blocks = []
        for h in range(heads):
            blocks += [(h, qi, True, None, zero), (h, qi - 1, False, 2 * h, None)]
        carries, outs = run_blocks(blocks)
        for h in range(heads):
            carry_ref[h] = carries[2 * h + 1]
            acc_ref[:, hcols[h]] = outs[2 * h] + outs[2 * h + 1]

    def more(state):
        kb, live = state
        return jnp.logical_and(kb >= 0, live)

    def body(state):
        kb, _ = state
        carries, outs = run_blocks([(h, kb, False, None, carry_ref[h]) for h in range(heads)])
        top = None
        for h in range(heads):
            carry_ref[h] = carries[h]
            acc_ref[:, hcols[h]] += outs[h]
            top = jnp.max(carries[h]) if top is None else jnp.maximum(top, jnp.max(carries[h]))
        return kb - 1, top > SB_LOG2_ZERO

    lax.while_loop(more, body, (qi - 2, jnp.max(carry_ref[...]) > SB_LOG2_ZERO))
    gate = gate_ref[...].astype(F32)
    o_ref[...] = (acc_ref[...] * (gate * _sigmoid(gate))).astype(BF16)


def _stick_breaking(u, *, batch, tq, heads):
    n = u.shape[0]
    seq = n // batch
    nq = seq // tq
    j = np.arange(tq)[:, None]
    s = np.arange(tq)[None, :]
    tri = jnp.asarray((j > s).astype(np.float32), BF16)
    width = heads * SB_DH
    rows = lambda b, g, i: b * nq + i
    return pl.pallas_call(
        functools.partial(_sb_kernel, tq=tq, heads=heads),
        grid=(batch, SB_HEADS // heads, nq),
        in_specs=[
            pl.BlockSpec((tq, width), lambda b, g, i: (rows(b, g, i), U_SQ // width + g)),
            pl.BlockSpec((seq, width), lambda b, g, i: (b, U_SK // width + g)),
            pl.BlockSpec((seq, width), lambda b, g, i: (b, U_SV // width + g)),
            pl.BlockSpec((tq, width), lambda b, g, i: (rows(b, g, i), U_SG // width + g)),
            pl.BlockSpec((tq, tq), lambda b, g, i: (0, 0)),
        ],
        out_specs=pl.BlockSpec((tq, width), lambda b, g, i: (rows(b, g, i), g)),
        out_shape=jax.ShapeDtypeStruct((n, D_SB), BF16),
        scratch_shapes=[pltpu.VMEM((heads, tq, 1), F32), pltpu.VMEM((tq, width), F32)],
        compiler_params=pltpu.CompilerParams(
            dimension_semantics=("parallel", "parallel", "arbitrary"),
            vmem_limit_bytes=VMEM_LIMIT_BYTES),
        name="stick_breaking",
    )(u, u, u, u, tri)


def _out_kernel(og_ref, os_ref, x_ref, p_ref, wo_ref, gpost_ref, wg_ref, bg_ref, wp_ref, o_ref):
    mix = _dot(og_ref[...], wo_ref[:D_GLA, :]) + _dot(os_ref[...], wo_ref[D_GLA:, :])
    ms = jnp.mean(mix * mix, axis=-1, keepdims=True)
    h1 = x_ref[...] + mix * lax.rsqrt(ms + EPS) * gpost_ref[...]
    gate = _sigmoid(_dot(h1.astype(BF16), wg_ref[...]) + bg_ref[...])
    emb = _dot(p_ref[...].astype(BF16), wp_ref[...])
    o_ref[...] = h1 + gate * emb


def _out_proj(o_gla, o_sb, x2, p2, w_out, g_post, w_gate, b_gate, w_proj, *, tm):
    n, d = x2.shape
    dp = p2.shape[1]
    dmix = w_out.shape[0]
    const = lambda i: (0, 0)
    return pl.pallas_call(
        _out_kernel,
        grid=(n // tm,),
        in_specs=[
            pl.BlockSpec((tm, D_GLA), lambda i: (i, 0)),
            pl.BlockSpec((tm, D_SB), lambda i: (i, 0)),
            pl.BlockSpec((tm, d), lambda i: (i, 0)),
            pl.BlockSpec((tm, dp), lambda i: (i, 0)),
            pl.BlockSpec((dmix, d), const, pipeline_mode=pl.Buffered(1)),
            pl.BlockSpec((1, d), const),
            pl.BlockSpec((d, d), const, pipeline_mode=pl.Buffered(1)),
            pl.BlockSpec((1, d), const),
            pl.BlockSpec((dp, d), const, pipeline_mode=pl.Buffered(1)),
        ],
        out_specs=pl.BlockSpec((tm, d), lambda i: (i, 0)),
        out_shape=jax.ShapeDtypeStruct((n, d), F32),
        compiler_params=pltpu.CompilerParams(
            dimension_semantics=("parallel",),
            vmem_limit_bytes=VMEM_LIMIT_BYTES),
        name="out_proj",
    )(o_gla, o_sb, x2, p2, w_out, g_post, w_gate, b_gate, w_proj)


def kernel(x, p, g_pre, w_in, w_a2, b_a, g_gla_head, w_out, g_post, w_ple_gate, b_ple_gate, w_ple_proj):
    batch, seq, d = x.shape
    n = batch * seq
    depth = w_in.shape[0]
    lr0 = U_SQ
    h_res = x.astype(F32).reshape(n, d)
    for i in range(depth):
        w = w_in[i]
        w_main = _w_prep(w, tr=256)
        w_lr = jnp.pad(w[:, lr0:lr0 + GLA_RANK], ((0, 0), (0, LANES - GLA_RANK))).astype(BF16)
        w_a2p = jnp.pad(w_a2[i], ((0, LANES - GLA_RANK), (0, 0))).astype(BF16)
        u, log_a = _in_proj(h_res, g_pre[i][None, :], w_main, w_lr, w_a2p, b_a[i][None, :],
                            tm=1024, tn=1024)
        o_gla = _gla(u, log_a, g_gla_head[i][None, :], batch=batch, tc=512)
        o_sb = _stick_breaking(u, batch=batch, tq=256, heads=8)
        h_res = _out_proj(o_gla, o_sb, h_res, p[i].astype(F32).reshape(n, -1),
                          w_out[i].astype(BF16), g_post[i][None, :],
                          w_ple_gate[i].astype(BF16), b_ple_gate[i][None, :],
                          w_ple_proj[i].astype(BF16), tm=512)
    return h_res.reshape(batch, seq, d).astype(x.dtype)
```

```python
import functools

import numpy as np
import jax
import jax.numpy as jnp
from jax import lax
from jax.experimental import pallas as pl
from jax.experimental.pallas import tpu as pltpu

F32 = jnp.float32
BF16 = jnp.bfloat16

EPS = 1e-6
GLA_HEADS = 4
GLA_DK = 128
GLA_DV = 256
GLA_RANK = 16
GLA_TAU = 16.0
GLA_CHUNK = 64
GLA_LEVELS = 6
GLA_GROUP = 4
SB_HEADS = 8
SB_DH = 128

LOG2_E = 1.4426950408889634
SB_LOG2_ZERO = -160.0
SB_SKEW = 1

LANES = 128
VMEM_LIMIT_BYTES = 56 * 1024 * 1024

D_QK = GLA_HEADS * GLA_DK
D_GLA = GLA_HEADS * GLA_DV
D_SB = SB_HEADS * SB_DH
U_GQ, U_GK, U_GV, U_GG = 0, D_QK, 2 * D_QK, 2 * D_QK + D_GLA
U_SQ = U_GG + D_GLA
U_SK, U_SV, U_SG = U_SQ + D_SB, U_SQ + 2 * D_SB, U_SQ + 3 * D_SB
D_U = U_SG + D_SB


def _nt_dot(a, b):
    return lax.dot_general(a, b, (((1,), (1,)), ((), ())), preferred_element_type=F32)


def _tn_dot(a, b):
    return lax.dot_general(a, b, (((0,), (0,)), ((), ())), preferred_element_type=F32)


def _dot(a, b):
    return jnp.dot(a, b, preferred_element_type=F32)


def _split_bf16(x):
    hi = x.astype(BF16)
    lo = (x - hi.astype(F32)).astype(BF16)
    return hi, lo


def _sigmoid(x):
    return 1.0 / (1.0 + jnp.exp(-x))


def _log1pexp_neg_abs(z):
    return jnp.log(1.0 + jnp.exp(-jnp.abs(z)))


def _w_prep_kernel(wt_ref, o_ref):
    o_ref[:U_SQ, :] = wt_ref[:U_SQ, :].astype(BF16)
    o_ref[U_SQ:, :] = wt_ref[U_SQ + GLA_RANK:, :].astype(BF16)


def _w_prep(wt, *, tc):
    d_in, d = wt.shape
    return pl.pallas_call(
        _w_prep_kernel,
        grid=(d // tc,),
        in_specs=[pl.BlockSpec((d_in, tc), lambda r: (0, r))],
        out_specs=pl.BlockSpec((D_U, tc), lambda r: (0, r)),
        out_shape=jax.ShapeDtypeStruct((D_U, d), BF16),
        compiler_params=pltpu.CompilerParams(
            dimension_semantics=("parallel",),
            vmem_limit_bytes=VMEM_LIMIT_BYTES),
        name="w_prep",
    )(wt)


def _in_proj_kernel(x_ref, g_ref, w_ref, wlr_ref, wa2_ref, ba_ref, u_ref, la_ref, h_scr):
    @pl.when(pl.program_id(1) == 0)
    def _():
        x = x_ref[...]
        ms = jnp.mean(x * x, axis=-1, keepdims=True)
        hb = (x * lax.rsqrt(ms + EPS) * g_ref[...]).astype(BF16)
        h_scr[...] = hb
        g_lr = _nt_dot(hb, wlr_ref[...])
        z = _dot(g_lr.astype(BF16), wa2_ref[...]) + ba_ref[...]
        la_ref[...] = (jnp.minimum(z, 0.0) - _log1pexp_neg_abs(z)) * (LOG2_E / GLA_TAU)

    u_ref[...] = _nt_dot(h_scr[...], w_ref[...]).astype(BF16)


def _in_proj(x2, g_pre, wt_main, wt_lr, w_a2, b_a, *, tm, tn):
    n, d = x2.shape
    du = wt_main.shape[0]
    return pl.pallas_call(
        _in_proj_kernel,
        grid=(n // tm, du // tn),
        in_specs=[
            pl.BlockSpec((tm, d), lambda i, j: (i, 0)),
            pl.BlockSpec((1, d), lambda i, j: (0, 0)),
            pl.BlockSpec((tn, d), lambda i, j: (j, 0)),
            pl.BlockSpec((LANES, d), lambda i, j: (0, 0)),
            pl.BlockSpec((LANES, D_QK), lambda i, j: (0, 0)),
            pl.BlockSpec((1, D_QK), lambda i, j: (0, 0)),
        ],
        out_specs=[
            pl.BlockSpec((tm, tn), lambda i, j: (i, j)),
            pl.BlockSpec((tm, D_QK), lambda i, j: (i, 0)),
        ],
        out_shape=[
            jax.ShapeDtypeStruct((n, du), BF16),
            jax.ShapeDtypeStruct((n, D_QK), F32),
        ],
        scratch_shapes=[pltpu.VMEM((tm, d), BF16)],
        compiler_params=pltpu.CompilerParams(
            dimension_semantics=("parallel", "arbitrary"),
            vmem_limit_bytes=VMEM_LIMIT_BYTES),
        name="in_proj",
    )(x2, g_pre, wt_main, wt_lr, w_a2, b_a)


def _gla_decay_matrix():
    c = GLA_CHUNK
    r = np.arange(c)[:, None]
    t = np.arange(c)[None, :]
    groups = []
    for lvl in range(GLA_LEVELS):
        s = 1 << lvl
        start = (r // (2 * s)) * (2 * s)
        m = start + s - 1
        upper = (r >= start + s) & (t > m) & (t <= r)
        lower = (r <= m) & (t > r) & (t <= m)
        groups.append(upper | lower)
    groups += [t <= r, t > r]
    a = np.concatenate(groups, axis=0).astype(np.float32)
    return np.concatenate([a, a], axis=1)


def _gla_level_matrix():
    c = GLA_CHUNK
    i = np.arange(c)[:, None]
    j = np.arange(c)[None, :]
    lvl = np.full((c, c), GLA_LEVELS + 1, np.int32)
    lvl[i == j] = GLA_LEVELS
    x = i ^ j
    for b in range(GLA_LEVELS):
        lvl[(i > j) & ((x >> b) == 1)] = b
    return lvl


def _gla_kernel(q_ref, k_ref, v_ref, gate_ref, la_ref, a_ref, lvl_ref, gh_ref, o_ref, st_ref,
                *, chunks_per_block):
    c = GLA_CHUNK

    @pl.when(pl.program_id(1) == 0)
    def _():
        st_ref[...] = jnp.zeros_like(st_ref)

    a_mat = a_ref[...]
    lvl = lvl_ref[...]
    g_head = gh_ref[...]
    scale = GLA_DK ** -0.5

    def group(gi, carry):
        rows = [pl.ds(pl.multiple_of((gi * GLA_GROUP + j) * c, c), c) for j in range(GLA_GROUP)]
        kcols = [slice(h * GLA_DK, (h + 1) * GLA_DK) for h in range(GLA_HEADS)]
        vcols = [slice(h * GLA_DV, (h + 1) * GLA_DV) for h in range(GLA_HEADS)]
        pairs = [(j, h) for j in range(GLA_GROUP) for h in range(GLA_HEADS)]

        decay = []
        for j in range(GLA_GROUP):
            la_parts = jnp.concatenate(_split_bf16(la_ref[rows[j], :]), axis=0)
            decay.append(jnp.exp2(_dot(a_mat, la_parts)))

        def dec(j, h, g):
            return decay[j][g * c:(g + 1) * c, kcols[h]]

        q = {(j, h): q_ref[rows[j], kcols[h]].astype(F32) * scale for j, h in pairs}
        k = {(j, h): k_ref[rows[j], kcols[h]].astype(F32) for j, h in pairs}

        scores = {p: jnp.where(lvl == GLA_LEVELS, _nt_dot(q[p].astype(BF16), k[p].astype(BF16)), 0.0)
                  for p in pairs}
        for l in range(GLA_LEVELS):
            for j, h in pairs:
                ql = (q[j, h] * dec(j, h, l)).astype(BF16)
                kl = (k[j, h] * dec(j, h, l)).astype(BF16)
                scores[j, h] = jnp.where(lvl == l, _nt_dot(ql, kl), scores[j, h])

        intra = {(j, h): _dot(scores[j, h].astype(BF16), v_ref[rows[j], vcols[h]]) for j, h in pairs}

        out = {}
        for j in range(GLA_GROUP):
            st = [st_ref[h] for h in range(GLA_HEADS)]
            for h in range(GLA_HEADS):
                d_b = dec(j, h, GLA_LEVELS)
                out[j, h] = intra[j, h] + _nt_dot((q[j, h] * d_b).astype(BF16), st[h].astype(BF16))
            for h in range(GLA_HEADS):
                d_b = dec(j, h, GLA_LEVELS)
                kd = (k[j, h] * dec(j, h, GLA_LEVELS + 1)).astype(BF16)
                st_ref[h] = st[h] * d_b[c - 1:c, :] + _tn_dot(v_ref[rows[j], vcols[h]], kd)

        for j, h in pairs:
            o = out[j, h]
            ms = jnp.mean(o * o, axis=-1, keepdims=True)
            gate = gate_ref[rows[j], vcols[h]].astype(F32)
            o = o * lax.rsqrt(ms + EPS) * g_head * (gate * _sigmoid(gate))
            o_ref[rows[j], vcols[h]] = o.astype(BF16)
        return carry

    lax.fori_loop(0, chunks_per_block // GLA_GROUP, group, 0)


def _gla(u, log_a, g_head, *, batch, tc):
    n = u.shape[0]
    seq = n // batch
    nblk = seq // tc
    assert seq % tc == 0 and tc % (GLA_CHUNK * GLA_GROUP) == 0, (seq, tc)
    a_mat = jnp.asarray(_gla_decay_matrix(), BF16)
    lvl = jnp.asarray(_gla_level_matrix())
    rows = lambda b, i: b * nblk + i
    return pl.pallas_call(
        functools.partial(_gla_kernel, chunks_per_block=tc // GLA_CHUNK),
        grid=(batch, nblk),
        in_specs=[
            pl.BlockSpec((tc, D_QK), lambda b, i: (rows(b, i), U_GQ // D_QK)),
            pl.BlockSpec((tc, D_QK), lambda b, i: (rows(b, i), U_GK // D_QK)),
            pl.BlockSpec((tc, D_GLA), lambda b, i: (rows(b, i), U_GV // D_GLA)),
            pl.BlockSpec((tc, D_GLA), lambda b, i: (rows(b, i), U_GG // D_GLA)),
            pl.BlockSpec((tc, D_QK), lambda b, i: (rows(b, i), 0)),
            pl.BlockSpec(a_mat.shape, lambda b, i: (0, 0)),
            pl.BlockSpec(lvl.shape, lambda b, i: (0, 0)),
            pl.BlockSpec((1, GLA_DV), lambda b, i: (0, 0)),
        ],
        out_specs=pl.BlockSpec((tc, D_GLA), lambda b, i: (rows(b, i), 0)),
        out_shape=jax.ShapeDtypeStruct((n, D_GLA), BF16),
        scratch_shapes=[pltpu.VMEM((GLA_HEADS, GLA_DV, GLA_DK), F32)],
        compiler_params=pltpu.CompilerParams(
            dimension_semantics=("parallel", "arbitrary"),
            vmem_limit_bytes=VMEM_LIMIT_BYTES),
        name="gla",
    )(u, u, u, u, log_a, a_mat, lvl, g_head)


def _sb_kernel(q_ref, k_ref, v_ref, gate_ref, tri_ref, o_ref, carry_ref, acc_ref, *, tq, heads):
    qi = pl.program_id(2)
    tri = tri_ref[...]
    row = lax.broadcasted_iota(jnp.int32, (tq, tq), 0)
    col = lax.broadcasted_iota(jnp.int32, (tq, tq), 1)
    before = col < row
    hcols = [slice(h * SB_DH, (h + 1) * SB_DH) for h in range(heads)]
    qn = [(q_ref[:, c].astype(F32) * (-(SB_DH ** -0.5) * LOG2_E)).astype(BF16) for c in hcols]

    def logits_stage(h, kb, diagonal):
        ks = pl.ds(pl.multiple_of(kb * tq, tq), tq)
        y = _nt_dot(qn[h], k_ref[ks, hcols[h]])
        soft = jnp.log2(1.0 + jnp.exp2(-jnp.abs(y)))
        log_keep = jnp.minimum(y, 0.0) - soft
        if diagonal:
            log_keep = jnp.where(before, log_keep, 0.0)
        return log_keep - y, log_keep.astype(BF16), jnp.sum(log_keep, axis=-1, keepdims=True)

    def weights_stage(log_beta, keep_bf16, carry, diagonal):
        suffix = _dot(keep_bf16, tri)
        w = jnp.exp2(log_beta + suffix + carry)
        if diagonal:
            w = jnp.where(before, w, 0.0)
        return w.astype(BF16)

    def output_stage(h, kb, w):
        ks = pl.ds(pl.multiple_of(kb * tq, tq), tq)
        return _dot(w, v_ref[ks, hcols[h]])

    def run_blocks(blocks):
        n = len(blocks)
        stage_a, stage_b, carries, outs = [None] * n, [None] * n, [None] * n, [None] * n
        for t in range(n + 2 * SB_SKEW):
            if t < n:
                h, kb, diagonal, _, _ = blocks[t]
                stage_a[t] = logits_stage(h, kb, diagonal)
            b = t - SB_SKEW
            if 0 <= b < n:
                h, kb, diagonal, prev, carry = blocks[b]
                if prev is not None:
                    carry = carries[prev]
                log_beta, keep_bf16, row_sum = stage_a[b]
                stage_b[b] = weights_stage(log_beta, keep_bf16, carry, diagonal)
                carries[b] = carry + row_sum
            c = t - 2 * SB_SKEW
            if 0 <= c < n:
                h, kb, _, _, _ = blocks[c]
                outs[c] = output_stage(h, kb, stage_b[c])
        return carries, outs

    zero = jnp.zeros((tq, 1), F32)

    @pl.when(qi == 0)
    def _():
        _, outs = run_blocks([(h, 0, True, None, zero) for h in range(heads)])
        for h in range(heads):
            acc_ref[:, hcols[h]] = outs[h]
        carry_ref[...] = jnp.full(carry_ref.shape, SB_LOG2_ZERO, F32)

    @pl.when(qi > 0)
    def _():
        blocks = []
        for h in range(heads):
            blocks += [(h, qi, True, None, zero), (h, qi - 1, False, 2 * h, None)]
        carries, outs = run_blocks(blocks)
        for h in range(heads):
            carry_ref[h] = carries[2 * h + 1]
            acc_ref[:, hcols[h]] = outs[2 * h] + outs[2 * h + 1]

    def more(state):
        kb, live = state
        return jnp.logical_and(kb >= 0, live)

    def body(state):
        kb, _ = state
        carries, outs = run_blocks([(h, kb, False, None, carry_ref[h]) for h in range(heads)])
        top = None
        for h in range(heads):
            carry_ref[h] = carries[h]
            acc_ref[:, hcols[h]] += outs[h]
            top = jnp.max(carries[h]) if top is None else jnp.maximum(top, jnp.max(carries[h]))
        return kb - 1, top > SB_LOG2_ZERO

    lax.while_loop(more, body, (qi - 2, jnp.max(carry_ref[...]) > SB_LOG2_ZERO))
    gate = gate_ref[...].astype(F32)
    o_ref[...] = (acc_ref[...] * (gate * _sigmoid(gate))).astype(BF16)


def _stick_breaking(u, *, batch, tq, heads):
    n = u.shape[0]
    seq = n // batch
    nq = seq // tq
    j = np.arange(tq)[:, None]
    s = np.arange(tq)[None, :]
    tri = jnp.asarray((j > s).astype(np.float32), BF16)
    width = heads * SB_DH
    rows = lambda b, g, i: b * nq + i
    return pl.pallas_call(
        functools.partial(_sb_kernel, tq=tq, heads=heads),
        grid=(batch, SB_HEADS // heads, nq),
        in_specs=[
            pl.BlockSpec((tq, width), lambda b, g, i: (rows(b, g, i), U_SQ // width + g)),
            pl.BlockSpec((seq, width), lambda b, g, i: (b, U_SK // width + g)),
            pl.BlockSpec((seq, width), lambda b, g, i: (b, U_SV // width + g)),
            pl.BlockSpec((tq, width), lambda b, g, i: (rows(b, g, i), U_SG // width + g)),
            pl.BlockSpec((tq, tq), lambda b, g, i: (0, 0)),
        ],
        out_specs=pl.BlockSpec((tq, width), lambda b, g, i: (rows(b, g, i), g)),
        out_shape=jax.ShapeDtypeStruct((n, D_SB), BF16),
        scratch_shapes=[pltpu.VMEM((heads, tq, 1), F32), pltpu.VMEM((tq, width), F32)],
        compiler_params=pltpu.CompilerParams(
            dimension_semantics=("parallel", "parallel", "arbitrary"),
            vmem_limit_bytes=VMEM_LIMIT_BYTES),
        name="stick_breaking",
    )(u, u, u, u, tri)


def _out_kernel(og_ref, os_ref, x_ref, p_ref, wo_ref, gpost_ref, wg_ref, bg_ref, wp_ref, o_ref):
    mix = _dot(og_ref[...], wo_ref[:D_GLA, :]) + _dot(os_ref[...], wo_ref[D_GLA:, :])
    ms = jnp.mean(mix * mix, axis=-1, keepdims=True)
    h1 = x_ref[...] + mix * lax.rsqrt(ms + EPS) * gpost_ref[...]
    gate = _sigmoid(_dot(h1.astype(BF16), wg_ref[...]) + bg_ref[...])
    emb = _dot(p_ref[...].astype(BF16), wp_ref[...])
    o_ref[...] = h1 + gate * emb


def _out_proj(o_gla, o_sb, x2, p2, w_out, g_post, w_gate, b_gate, w_proj, *, tm):
    n, d = x2.shape
    dp = p2.shape[1]
    dmix = w_out.shape[0]
    const = lambda i: (0, 0)
    return pl.pallas_call(
        _out_kernel,
        grid=(n // tm,),
        in_specs=[
            pl.BlockSpec((tm, D_GLA), lambda i: (i, 0)),
            pl.BlockSpec((tm, D_SB), lambda i: (i, 0)),
            pl.BlockSpec((tm, d), lambda i: (i, 0)),
            pl.BlockSpec((tm, dp), lambda i: (i, 0)),
            pl.BlockSpec((dmix, d), const, pipeline_mode=pl.Buffered(1)),
            pl.BlockSpec((1, d), const),
            pl.BlockSpec((d, d), const, pipeline_mode=pl.Buffered(1)),
            pl.BlockSpec((1, d), const),
            pl.BlockSpec((dp, d), const, pipeline_mode=pl.Buffered(1)),
        ],
        out_specs=pl.BlockSpec((tm, d), lambda i: (i, 0)),
        out_shape=jax.ShapeDtypeStruct((n, d), F32),
        compiler_params=pltpu.CompilerParams(
            dimension_semantics=("parallel",),
            vmem_limit_bytes=VMEM_LIMIT_BYTES),
        name="out_proj",
    )(o_gla, o_sb, x2, p2, w_out, g_post, w_gate, b_gate, w_proj)


def kernel(x, p, g_pre, w_in, w_a2, b_a, g_gla_head, w_out, g_post, w_ple_gate, b_ple_gate, w_ple_proj):
    batch, seq, d = x.shape
    n = batch * seq
    depth = w_in.shape[0]
    lr0 = U_SQ
    h_res = x.astype(F32).reshape(n, d)
    for i in range(depth):
        wt = jnp.transpose(w_in[i])
        wt_main = _w_prep(wt, tc=256)
        wt_lr = jnp.pad(wt[lr0:lr0 + GLA_RANK], ((0, LANES - GLA_RANK), (0, 0))).astype(BF16)
        w_a2p = jnp.pad(w_a2[i], ((0, LANES - GLA_RANK), (0, 0))).astype(BF16)
        u, log_a = _in_proj(h_res, g_pre[i][None, :], wt_main, wt_lr, w_a2p, b_a[i][None, :],
                            tm=1024, tn=1024)
        o_gla = _gla(u, log_a, g_gla_head[i][None, :], batch=batch, tc=512)
        o_sb = _stick_breaking(u, batch=batch, tq=256, heads=8)
        h_res = _out_proj(o_gla, o_sb, h_res, p[i].astype(F32).reshape(n, -1),
                          w_out[i].astype(BF16), g_post[i][None, :],
                          w_ple_gate[i].astype(BF16), b_ple_gate[i][None, :],
                          w_ple_proj[i].astype(BF16), tm=512)
    return h_res.reshape(batch, seq, d).astype(x.dtype)
```

```python
import functools

import numpy as np
import jax
import jax.numpy as jnp
from jax import lax
from jax.experimental import pallas as pl
from jax.experimental.pallas import tpu as pltpu

F32 = jnp.float32
BF16 = jnp.bfloat16

EPS = 1e-6
GLA_HEADS = 4
GLA_DK = 128
GLA_DV = 256
GLA_RANK = 16
GLA_TAU = 16.0
GLA_CHUNK = 64
GLA_LEVELS = 6
GLA_GROUP = 4
SB_HEADS = 8
SB_DH = 128

LOG2_E = 1.4426950408889634
SB_LOG2_ZERO = -160.0
SB_SKEW = 1
SB_ROW_SPLIT = 1

LANES = 128
VMEM_LIMIT_BYTES = 56 * 1024 * 1024

D_QK = GLA_HEADS * GLA_DK
D_GLA = GLA_HEADS * GLA_DV
D_SB = SB_HEADS * SB_DH
U_GQ, U_GK, U_GV, U_GG = 0, D_QK, 2 * D_QK, 2 * D_QK + D_GLA
U_SQ = U_GG + D_GLA
U_SK, U_SV, U_SG = U_SQ + D_SB, U_SQ + 2 * D_SB, U_SQ + 3 * D_SB
D_U = U_SG + D_SB


def _nt_dot(a, b):
    return lax.dot_general(a, b, (((1,), (1,)), ((), ())), preferred_element_type=F32)


def _tn_dot(a, b):
    return lax.dot_general(a, b, (((0,), (0,)), ((), ())), preferred_element_type=F32)


def _dot(a, b):
    return jnp.dot(a, b, preferred_element_type=F32)


def _split_bf16(x):
    hi = x.astype(BF16)
    lo = (x - hi.astype(F32)).astype(BF16)
    return hi, lo


def _sigmoid(x):
    return 1.0 / (1.0 + jnp.exp(-x))


def _log1pexp_neg_abs(z):
    return jnp.log(1.0 + jnp.exp(-jnp.abs(z)))


def _w_prep_kernel(wt_ref, o_ref):
    o_ref[:U_SQ, :] = wt_ref[:U_SQ, :].astype(BF16)
    o_ref[U_SQ:, :] = wt_ref[U_SQ + GLA_RANK:, :].astype(BF16)


def _w_prep(wt, *, tc):
    d_in, d = wt.shape
    return pl.pallas_call(
        _w_prep_kernel,
        grid=(d // tc,),
        in_specs=[pl.BlockSpec((d_in, tc), lambda r: (0, r))],
        out_specs=pl.BlockSpec((D_U, tc), lambda r: (0, r)),
        out_shape=jax.ShapeDtypeStruct((D_U, d), BF16),
        compiler_params=pltpu.CompilerParams(
            dimension_semantics=("parallel",),
            vmem_limit_bytes=VMEM_LIMIT_BYTES),
        name="w_prep",
    )(wt)


def _in_proj_kernel(x_ref, g_ref, w_ref, wlr_ref, wa2_ref, ba_ref, u_ref, la_ref, h_scr):
    @pl.when(pl.program_id(1) == 0)
    def _():
        x = x_ref[...]
        ms = jnp.mean(x * x, axis=-1, keepdims=True)
        hb = (x * lax.rsqrt(ms + EPS) * g_ref[...]).astype(BF16)
        h_scr[...] = hb
        g_lr = _nt_dot(hb, wlr_ref[...])
        z = _dot(g_lr.astype(BF16), wa2_ref[...]) + ba_ref[...]
        la_ref[...] = (jnp.minimum(z, 0.0) - _log1pexp_neg_abs(z)) * (LOG2_E / GLA_TAU)

    u_ref[...] = _nt_dot(h_scr[...], w_ref[...]).astype(BF16)


def _in_proj(x2, g_pre, wt_main, wt_lr, w_a2, b_a, *, tm, tn):
    n, d = x2.shape
    du = wt_main.shape[0]
    return pl.pallas_call(
        _in_proj_kernel,
        grid=(n // tm, du // tn),
        in_specs=[
            pl.BlockSpec((tm, d), lambda i, j: (i, 0)),
            pl.BlockSpec((1, d), lambda i, j: (0, 0)),
            pl.BlockSpec((tn, d), lambda i, j: (j, 0)),
            pl.BlockSpec((LANES, d), lambda i, j: (0, 0)),
            pl.BlockSpec((LANES, D_QK), lambda i, j: (0, 0)),
            pl.BlockSpec((1, D_QK), lambda i, j: (0, 0)),
        ],
        out_specs=[
            pl.BlockSpec((tm, tn), lambda i, j: (i, j)),
            pl.BlockSpec((tm, D_QK), lambda i, j: (i, 0)),
        ],
        out_shape=[
            jax.ShapeDtypeStruct((n, du), BF16),
            jax.ShapeDtypeStruct((n, D_QK), F32),
        ],
        scratch_shapes=[pltpu.VMEM((tm, d), BF16)],
        compiler_params=pltpu.CompilerParams(
            dimension_semantics=("parallel", "arbitrary"),
            vmem_limit_bytes=VMEM_LIMIT_BYTES),
        name="in_proj",
    )(x2, g_pre, wt_main, wt_lr, w_a2, b_a)


def _gla_decay_matrix():
    c = GLA_CHUNK
    r = np.arange(c)[:, None]
    t = np.arange(c)[None, :]
    groups = []
    for lvl in range(GLA_LEVELS):
        s = 1 << lvl
        start = (r // (2 * s)) * (2 * s)
        m = start + s - 1
        upper = (r >= start + s) & (t > m) & (t <= r)
        lower = (r <= m) & (t > r) & (t <= m)
        groups.append(upper | lower)
    groups += [t <= r, t > r]
    a = np.concatenate(groups, axis=0).astype(np.float32)
    return np.concatenate([a, a], axis=1)


def _gla_level_matrix():
    c = GLA_CHUNK
    i = np.arange(c)[:, None]
    j = np.arange(c)[None, :]
    lvl = np.full((c, c), GLA_LEVELS + 1, np.int32)
    lvl[i == j] = GLA_LEVELS
    x = i ^ j
    for b in range(GLA_LEVELS):
        lvl[(i > j) & ((x >> b) == 1)] = b
    return lvl


def _gla_kernel(q_ref, k_ref, v_ref, gate_ref, la_ref, a_ref, lvl_ref, gh_ref, o_ref, st_ref,
                *, chunks_per_block):
    c = GLA_CHUNK

    @pl.when(pl.program_id(1) == 0)
    def _():
        st_ref[...] = jnp.zeros_like(st_ref)

    a_mat = a_ref[...]
    lvl = lvl_ref[...]
    g_head = gh_ref[...]
    scale = GLA_DK ** -0.5

    def group(gi, carry):
        rows = [pl.ds(pl.multiple_of((gi * GLA_GROUP + j) * c, c), c) for j in range(GLA_GROUP)]
        kcols = [slice(h * GLA_DK, (h + 1) * GLA_DK) for h in range(GLA_HEADS)]
        vcols = [slice(h * GLA_DV, (h + 1) * GLA_DV) for h in range(GLA_HEADS)]
        pairs = [(j, h) for j in range(GLA_GROUP) for h in range(GLA_HEADS)]

        decay = []
        for j in range(GLA_GROUP):
            la_parts = jnp.concatenate(_split_bf16(la_ref[rows[j], :]), axis=0)
            decay.append(jnp.exp2(_dot(a_mat, la_parts)))

        def dec(j, h, g):
            return decay[j][g * c:(g + 1) * c, kcols[h]]

        q = {(j, h): q_ref[rows[j], kcols[h]].astype(F32) * scale for j, h in pairs}
        k = {(j, h): k_ref[rows[j], kcols[h]].astype(F32) for j, h in pairs}

        scores = {p: jnp.where(lvl == GLA_LEVELS, _nt_dot(q[p].astype(BF16), k[p].astype(BF16)), 0.0)
                  for p in pairs}
        for l in range(GLA_LEVELS):
            for j, h in pairs:
                ql = (q[j, h] * dec(j, h, l)).astype(BF16)
                kl = (k[j, h] * dec(j, h, l)).astype(BF16)
                scores[j, h] = jnp.where(lvl == l, _nt_dot(ql, kl), scores[j, h])

        intra = {(j, h): _dot(scores[j, h].astype(BF16), v_ref[rows[j], vcols[h]]) for j, h in pairs}

        out = {}
        for j in range(GLA_GROUP):
            st = [st_ref[h] for h in range(GLA_HEADS)]
            for h in range(GLA_HEADS):
                d_b = dec(j, h, GLA_LEVELS)
                out[j, h] = intra[j, h] + _nt_dot((q[j, h] * d_b).astype(BF16), st[h].astype(BF16))
            for h in range(GLA_HEADS):
                d_b = dec(j, h, GLA_LEVELS)
                kd = (k[j, h] * dec(j, h, GLA_LEVELS + 1)).astype(BF16)
                st_ref[h] = st[h] * d_b[c - 1:c, :] + _tn_dot(v_ref[rows[j], vcols[h]], kd)

        for j, h in pairs:
            o = out[j, h]
            ms = jnp.mean(o * o, axis=-1, keepdims=True)
            gate = gate_ref[rows[j], vcols[h]].astype(F32)
            o = o * lax.rsqrt(ms + EPS) * g_head * (gate * _sigmoid(gate))
            o_ref[rows[j], vcols[h]] = o.astype(BF16)
        return carry

    lax.fori_loop(0, chunks_per_block // GLA_GROUP, group, 0)


def _gla(u, log_a, g_head, *, batch, tc):
    n = u.shape[0]
    seq = n // batch
    nblk = seq // tc
    assert seq % tc == 0 and tc % (GLA_CHUNK * GLA_GROUP) == 0, (seq, tc)
    a_mat = jnp.asarray(_gla_decay_matrix(), BF16)
    lvl = jnp.asarray(_gla_level_matrix())
    rows = lambda b, i: b * nblk + i
    return pl.pallas_call(
        functools.partial(_gla_kernel, chunks_per_block=tc // GLA_CHUNK),
        grid=(batch, nblk),
        in_specs=[
            pl.BlockSpec((tc, D_QK), lambda b, i: (rows(b, i), U_GQ // D_QK)),
            pl.BlockSpec((tc, D_QK), lambda b, i: (rows(b, i), U_GK // D_QK)),
            pl.BlockSpec((tc, D_GLA), lambda b, i: (rows(b, i), U_GV // D_GLA)),
            pl.BlockSpec((tc, D_GLA), lambda b, i: (rows(b, i), U_GG // D_GLA)),
            pl.BlockSpec((tc, D_QK), lambda b, i: (rows(b, i), 0)),
            pl.BlockSpec(a_mat.shape, lambda b, i: (0, 0)),
            pl.BlockSpec(lvl.shape, lambda b, i: (0, 0)),
            pl.BlockSpec((1, GLA_DV), lambda b, i: (0, 0)),
        ],
        out_specs=pl.BlockSpec((tc, D_GLA), lambda b, i: (rows(b, i), 0)),
        out_shape=jax.ShapeDtypeStruct((n, D_GLA), BF16),
        scratch_shapes=[pltpu.VMEM((GLA_HEADS, GLA_DV, GLA_DK), F32)],
        compiler_params=pltpu.CompilerParams(
            dimension_semantics=("parallel", "arbitrary"),
            vmem_limit_bytes=VMEM_LIMIT_BYTES),
        name="gla",
    )(u, u, u, u, log_a, a_mat, lvl, g_head)


def _sb_kernel(q_ref, k_ref, v_ref, gate_ref, tri_ref, o_ref, carry_ref, acc_ref, *, tq, heads):
    qi = pl.program_id(2)
    tri = tri_ref[...]
    tr = tq // SB_ROW_SPLIT
    rsl = [slice(r * tr, (r + 1) * tr) for r in range(SB_ROW_SPLIT)]
    hcols = [slice(h * SB_DH, (h + 1) * SB_DH) for h in range(heads)]
    units = [(h, r) for h in range(heads) for r in range(SB_ROW_SPLIT)]
    qn = {(h, r): (q_ref[rsl[r], hcols[h]].astype(F32) * (-(SB_DH ** -0.5) * LOG2_E)).astype(BF16)
          for h, r in units}

    def diag_keys(r):
        return (r + 1) * tr

    before = []
    for r in range(SB_ROW_SPLIT):
        row = lax.broadcasted_iota(jnp.int32, (tr, diag_keys(r)), 0) + r * tr
        col = lax.broadcasted_iota(jnp.int32, (tr, diag_keys(r)), 1)
        before.append(col < row)

    def logits_stage(h, r, kb, diagonal):
        nk = diag_keys(r) if diagonal else tq
        ks = pl.ds(pl.multiple_of(kb * tq, tq), nk)
        y = _nt_dot(qn[h, r], k_ref[ks, hcols[h]])
        soft = jnp.log2(1.0 + jnp.exp2(-jnp.abs(y)))
        log_keep = jnp.minimum(y, 0.0) - soft
        if diagonal:
            log_keep = jnp.where(before[r], log_keep, 0.0)
        return log_keep - y, log_keep.astype(BF16), jnp.sum(log_keep, axis=-1, keepdims=True)

    def weights_stage(log_beta, keep_bf16, carry, r, diagonal):
        nk = diag_keys(r) if diagonal else tq
        suffix = _dot(keep_bf16, tri[:nk, :nk])
        if diagonal:
            return jnp.where(before[r], jnp.exp2(log_beta + suffix), 0.0).astype(BF16)
        return jnp.exp2(log_beta + suffix + carry).astype(BF16)

    def output_stage(h, r, kb, diagonal, w):
        nk = diag_keys(r) if diagonal else tq
        ks = pl.ds(pl.multiple_of(kb * tq, tq), nk)
        return _dot(w, v_ref[ks, hcols[h]])

    def run_blocks(blocks):
        n = len(blocks)
        stage_a, stage_b, carries, outs = [None] * n, [None] * n, [None] * n, [None] * n
        for t in range(n + 2 * SB_SKEW):
            if t < n:
                h, r, kb, diagonal, _, _ = blocks[t]
                stage_a[t] = logits_stage(h, r, kb, diagonal)
            b = t - SB_SKEW
            if 0 <= b < n:
                h, r, kb, diagonal, prev, carry = blocks[b]
                if prev is not None:
                    carry = carries[prev]
                log_beta, keep_bf16, row_sum = stage_a[b]
                stage_b[b] = weights_stage(log_beta, keep_bf16, carry, r, diagonal)
                carries[b] = row_sum if diagonal else carry + row_sum
            c = t - 2 * SB_SKEW
            if 0 <= c < n:
                h, r, kb, diagonal, _, _ = blocks[c]
                outs[c] = output_stage(h, r, kb, diagonal, stage_b[c])
        return carries, outs

    @pl.when(qi == 0)
    def _():
        _, outs = run_blocks([(h, r, 0, True, None, None) for h, r in units])
        for i, (h, r) in enumerate(units):
            acc_ref[rsl[r], hcols[h]] = outs[i]
        carry_ref[...] = jnp.full(carry_ref.shape, SB_LOG2_ZERO, F32)

    @pl.when(qi > 0)
    def _():
        blocks = []
        for i, (h, r) in enumerate(units):
            blocks += [(h, r, qi, True, None, None), (h, r, qi - 1, False, 2 * i, None)]
        carries, outs = run_blocks(blocks)
        for i, (h, r) in enumerate(units):
            carry_ref[h, rsl[r], :] = carries[2 * i + 1]
            acc_ref[rsl[r], hcols[h]] = outs[2 * i] + outs[2 * i + 1]

    def more(state):
        kb, live = state
        return jnp.logical_and(kb >= 0, live)

    def body(state):
        kb, _ = state
        carries, outs = run_blocks([(h, r, kb, False, None, carry_ref[h, rsl[r], :]) for h, r in units])
        top = None
        for i, (h, r) in enumerate(units):
            carry_ref[h, rsl[r], :] = carries[i]
            acc_ref[rsl[r], hcols[h]] += outs[i]
            top = jnp.max(carries[i]) if top is None else jnp.maximum(top, jnp.max(carries[i]))
        return kb - 1, top > SB_LOG2_ZERO

    lax.while_loop(more, body, (qi - 2, jnp.max(carry_ref[...]) > SB_LOG2_ZERO))
    gate = gate_ref[...].astype(F32)
    o_ref[...] = (acc_ref[...] * (gate * _sigmoid(gate))).astype(BF16)


def _stick_breaking(u, *, batch, tq, heads):
    n = u.shape[0]
    seq = n // batch
    nq = seq // tq
    j = np.arange(tq)[:, None]
    s = np.arange(tq)[None, :]
    tri = jnp.asarray((j > s).astype(np.float32), BF16)
    width = heads * SB_DH
    rows = lambda b, g, i: b * nq + i
    return pl.pallas_call(
        functools.partial(_sb_kernel, tq=tq, heads=heads),
        grid=(batch, SB_HEADS // heads, nq),
        in_specs=[
            pl.BlockSpec((tq, width), lambda b, g, i: (rows(b, g, i), U_SQ // width + g)),
            pl.BlockSpec((seq, width), lambda b, g, i: (b, U_SK // width + g)),
            pl.BlockSpec((seq, width), lambda b, g, i: (b, U_SV // width + g)),
            pl.BlockSpec((tq, width), lambda b, g, i: (rows(b, g, i), U_SG // width + g)),
            pl.BlockSpec((tq, tq), lambda b, g, i: (0, 0)),
        ],
        out_specs=pl.BlockSpec((tq, width), lambda b, g, i: (rows(b, g, i), g)),
        out_shape=jax.ShapeDtypeStruct((n, D_SB), BF16),
        scratch_shapes=[pltpu.VMEM((heads, tq, 1), F32), pltpu.VMEM((tq, width), F32)],
        compiler_params=pltpu.CompilerParams(
            dimension_semantics=("parallel", "parallel", "arbitrary"),
            vmem_limit_bytes=VMEM_LIMIT_BYTES),
        name="stick_breaking",
    )(u, u, u, u, tri)


def _out_kernel(og_ref, os_ref, x_ref, p_ref, wo_ref, gpost_ref, wg_ref, bg_ref, wp_ref, o_ref):
    mix = _dot(og_ref[...], wo_ref[:D_GLA, :]) + _dot(os_ref[...], wo_ref[D_GLA:, :])
    ms = jnp.mean(mix * mix, axis=-1, keepdims=True)
    h1 = x_ref[...] + mix * lax.rsqrt(ms + EPS) * gpost_ref[...]
    gate = _sigmoid(_dot(h1.astype(BF16), wg_ref[...]) + bg_ref[...])
    emb = _dot(p_ref[...].astype(BF16), wp_ref[...])
    o_ref[...] = h1 + gate * emb


def _out_proj(o_gla, o_sb, x2, p2, w_out, g_post, w_gate, b_gate, w_proj, *, tm):
    n, d = x2.shape
    dp = p2.shape[1]
    dmix = w_out.shape[0]
    const = lambda i: (0, 0)
    return pl.pallas_call(
        _out_kernel,
        grid=(n // tm,),
        in_specs=[
            pl.BlockSpec((tm, D_GLA), lambda i: (i, 0)),
            pl.BlockSpec((tm, D_SB), lambda i: (i, 0)),
            pl.BlockSpec((tm, d), lambda i: (i, 0)),
            pl.BlockSpec((tm, dp), lambda i: (i, 0)),
            pl.BlockSpec((dmix, d), const, pipeline_mode=pl.Buffered(1)),
            pl.BlockSpec((1, d), const),
            pl.BlockSpec((d, d), const, pipeline_mode=pl.Buffered(1)),
            pl.BlockSpec((1, d), const),
            pl.BlockSpec((dp, d), const, pipeline_mode=pl.Buffered(1)),
        ],
        out_specs=pl.BlockSpec((tm, d), lambda i: (i, 0)),
        out_shape=jax.ShapeDtypeStruct((n, d), F32),
        compiler_params=pltpu.CompilerParams(
            dimension_semantics=("parallel",),
            vmem_limit_bytes=VMEM_LIMIT_BYTES),
        name="out_proj",
    )(o_gla, o_sb, x2, p2, w_out, g_post, w_gate, b_gate, w_proj)


def kernel(x, p, g_pre, w_in, w_a2, b_a, g_gla_head, w_out, g_post, w_ple_gate, b_ple_gate, w_ple_proj):
    batch, seq, d = x.shape
    n = batch * seq
    depth = w_in.shape[0]
    lr0 = U_SQ
    h_res = x.astype(F32).reshape(n, d)
    for i in range(depth):
        wt = jnp.transpose(w_in[i])
        wt_main = _w_prep(wt, tc=256)
        wt_lr = jnp.pad(wt[lr0:lr0 + GLA_RANK], ((0, LANES - GLA_RANK), (0, 0))).astype(BF16)
        w_a2p = jnp.pad(w_a2[i], ((0, LANES - GLA_RANK), (0, 0))).astype(BF16)
        u, log_a = _in_proj(h_res, g_pre[i][None, :], wt_main, wt_lr, w_a2p, b_a[i][None, :],
                            tm=1024, tn=1792)
        o_gla = _gla(u, log_a, g_gla_head[i][None, :], batch=batch, tc=1024)
        o_sb = _stick_breaking(u, batch=batch, tq=256, heads=8)
        h_res = _out_proj(o_gla, o_sb, h_res, p[i].astype(F32).reshape(n, -1),
                          w_out[i].astype(BF16), g_post[i][None, :],
                          w_ple_gate[i].astype(BF16), b_ple_gate[i][None, :],
                          w_ple_proj[i].astype(BF16), tm=512)
    return h_res.reshape(batch, seq, d).astype(x.dtype)
```

```python
import functools

import numpy as np
import jax
import jax.numpy as jnp
from jax import lax
from jax.experimental import pallas as pl
from jax.experimental.pallas import tpu as pltpu

F32 = jnp.float32
BF16 = jnp.bfloat16

EPS = 1e-6
GLA_HEADS = 4
GLA_DK = 128
GLA_DV = 256
GLA_RANK = 16
GLA_TAU = 16.0
GLA_CHUNK = 64
GLA_LEVELS = 6
GLA_GROUP = 4
SB_HEADS = 8
SB_DH = 128

LOG2_E = 1.4426950408889634
SB_LOG2_ZERO = -160.0
SB_SKEW = 1
SB_NEG_BIG = -1e30
SB_PROJ_CHUNK = 256

LANES = 128
VMEM_LIMIT_BYTES = 60 * 1024 * 1024

D_QK = GLA_HEADS * GLA_DK
D_GLA = GLA_HEADS * GLA_DV
D_SB = SB_HEADS * SB_DH
U_GQ, U_GK, U_GV, U_GG = 0, D_QK, 2 * D_QK, 2 * D_QK + D_GLA
U_SQ = U_GG + D_GLA
U_SK, U_SV, U_SG = U_SQ + D_SB, U_SQ + 2 * D_SB, U_SQ + 3 * D_SB
D_U = U_SG + D_SB


def _nt_dot(a, b):
    return lax.dot_general(a, b, (((1,), (1,)), ((), ())), preferred_element_type=F32)


def _tn_dot(a, b):
    return lax.dot_general(a, b, (((0,), (0,)), ((), ())), preferred_element_type=F32)


def _dot(a, b):
    return jnp.dot(a, b, preferred_element_type=F32)


def _split_bf16(x):
    hi = x.astype(BF16)
    lo = (x - hi.astype(F32)).astype(BF16)
    return hi, lo


def _sigmoid(x):
    return 1.0 / (1.0 + jnp.exp(-x))


def _log1pexp_neg_abs(z):
    return jnp.log(1.0 + jnp.exp(-jnp.abs(z)))


def _w_prep_kernel(wt_ref, o_ref):
    o_ref[:U_SQ, :] = wt_ref[:U_SQ, :].astype(BF16)
    o_ref[U_SQ:, :] = wt_ref[U_SQ + GLA_RANK:, :].astype(BF16)


def _w_prep(wt, *, tc):
    d_in, d = wt.shape
    return pl.pallas_call(
        _w_prep_kernel,
        grid=(d // tc,),
        in_specs=[pl.BlockSpec((d_in, tc), lambda r: (0, r))],
        out_specs=pl.BlockSpec((D_U, tc), lambda r: (0, r)),
        out_shape=jax.ShapeDtypeStruct((D_U, d), BF16),
        compiler_params=pltpu.CompilerParams(
            dimension_semantics=("parallel",),
            vmem_limit_bytes=VMEM_LIMIT_BYTES),
        name="w_prep",
    )(wt)


def _in_proj_kernel(x_ref, g_ref, w_ref, wlr_ref, wa2_ref, ba_ref, u_ref, la_ref, h_scr, glr_scr,
                    *, col_steps):
    j = pl.program_id(1)

    @pl.when(j == 0)
    def _():
        x = x_ref[...]
        ms = jnp.mean(x * x, axis=-1, keepdims=True)
        hb = (x * lax.rsqrt(ms + EPS) * g_ref[...]).astype(BF16)
        h_scr[...] = hb
        glr_scr[...] = _nt_dot(hb, wlr_ref[...]).astype(BF16)

    tr = la_ref.shape[0] // col_steps
    rows = pl.ds(pl.multiple_of(j * tr, tr), tr)
    z = _dot(glr_scr[rows, :], wa2_ref[...]) + ba_ref[...]
    la_ref[rows, :] = (jnp.minimum(z, 0.0) - _log1pexp_neg_abs(z)) * (LOG2_E / GLA_TAU)

    u_ref[...] = _nt_dot(h_scr[...], w_ref[...]).astype(BF16)


def _in_proj(x2, g_pre, wt_main, wt_lr, w_a2, b_a, *, tm, tn):
    n, d = x2.shape
    du = wt_main.shape[0]
    col_steps = du // tn
    assert du % tn == 0 and n % tm == 0 and tm % (16 * col_steps) == 0, (n, du, tm, tn)
    return pl.pallas_call(
        functools.partial(_in_proj_kernel, col_steps=col_steps),
        grid=(n // tm, col_steps),
        in_specs=[
            pl.BlockSpec((tm, d), lambda i, j: (i, 0)),
            pl.BlockSpec((1, d), lambda i, j: (0, 0)),
            pl.BlockSpec((tn, d), lambda i, j: (j, 0)),
            pl.BlockSpec((LANES, d), lambda i, j: (0, 0)),
            pl.BlockSpec((LANES, D_QK), lambda i, j: (0, 0)),
            pl.BlockSpec((1, D_QK), lambda i, j: (0, 0)),
        ],
        out_specs=[
            pl.BlockSpec((tm, tn), lambda i, j: (i, j)),
            pl.BlockSpec((tm, D_QK), lambda i, j: (i, 0)),
        ],
        out_shape=[
            jax.ShapeDtypeStruct((n, du), BF16),
            jax.ShapeDtypeStruct((n, D_QK), F32),
        ],
        scratch_shapes=[pltpu.VMEM((tm, d), BF16), pltpu.VMEM((tm, LANES), BF16)],
        compiler_params=pltpu.CompilerParams(
            dimension_semantics=("parallel", "arbitrary"),
            vmem_limit_bytes=VMEM_LIMIT_BYTES),
        name="in_proj",
    )(x2, g_pre, wt_main, wt_lr, w_a2, b_a)


def _gla_decay_matrix():
    c = GLA_CHUNK
    r = np.arange(c)[:, None]
    t = np.arange(c)[None, :]
    groups = []
    for lvl in range(GLA_LEVELS):
        s = 1 << lvl
        start = (r // (2 * s)) * (2 * s)
        m = start + s - 1
        upper = (r >= start + s) & (t > m) & (t <= r)
        lower = (r <= m) & (t > r) & (t <= m)
        groups.append(upper | lower)
    groups += [t <= r, t > r]
    a = np.concatenate(groups, axis=0).astype(np.float32)
    return np.concatenate([a, a], axis=1)


def _gla_level_matrix():
    c = GLA_CHUNK
    i = np.arange(c)[:, None]
    j = np.arange(c)[None, :]
    lvl = np.full((c, c), GLA_LEVELS + 1, np.int32)
    lvl[i == j] = GLA_LEVELS
    x = i ^ j
    for b in range(GLA_LEVELS):
        lvl[(i > j) & ((x >> b) == 1)] = b
    return lvl


def _gla_kernel(q_ref, k_ref, v_ref, gate_ref, la_ref, a_ref, lvl_ref, gh_ref, o_ref, st_ref,
                *, chunks_per_block):
    c = GLA_CHUNK

    @pl.when(pl.program_id(1) == 0)
    def _():
        st_ref[...] = jnp.zeros_like(st_ref)

    a_mat = a_ref[...]
    lvl = lvl_ref[...]
    g_head = gh_ref[...]
    scale = GLA_DK ** -0.5

    def group(gi, carry):
        rows = [pl.ds(pl.multiple_of((gi * GLA_GROUP + j) * c, c), c) for j in range(GLA_GROUP)]
        kcols = [slice(h * GLA_DK, (h + 1) * GLA_DK) for h in range(GLA_HEADS)]
        vcols = [slice(h * GLA_DV, (h + 1) * GLA_DV) for h in range(GLA_HEADS)]
        pairs = [(j, h) for j in range(GLA_GROUP) for h in range(GLA_HEADS)]

        decay = []
        for j in range(GLA_GROUP):
            la_parts = jnp.concatenate(_split_bf16(la_ref[rows[j], :]), axis=0)
            decay.append(jnp.exp2(_dot(a_mat, la_parts)))

        def dec(j, h, g):
            return decay[j][g * c:(g + 1) * c, kcols[h]]

        q = {(j, h): q_ref[rows[j], kcols[h]].astype(F32) * scale for j, h in pairs}
        k = {(j, h): k_ref[rows[j], kcols[h]].astype(F32) for j, h in pairs}

        scores = {p: jnp.where(lvl == GLA_LEVELS, _nt_dot(q[p].astype(BF16), k[p].astype(BF16)), 0.0)
                  for p in pairs}
        for l in range(GLA_LEVELS):
            for j, h in pairs:
                ql = (q[j, h] * dec(j, h, l)).astype(BF16)
                kl = (k[j, h] * dec(j, h, l)).astype(BF16)
                scores[j, h] = jnp.where(lvl == l, _nt_dot(ql, kl), scores[j, h])

        intra = {(j, h): _dot(scores[j, h].astype(BF16), v_ref[rows[j], vcols[h]]) for j, h in pairs}

        out = {}
        for j in range(GLA_GROUP):
            st = [st_ref[h] for h in range(GLA_HEADS)]
            for h in range(GLA_HEADS):
                d_b = dec(j, h, GLA_LEVELS)
                out[j, h] = intra[j, h] + _nt_dot((q[j, h] * d_b).astype(BF16), st[h].astype(BF16))
            for h in range(GLA_HEADS):
                d_b = dec(j, h, GLA_LEVELS)
                kd = (k[j, h] * dec(j, h, GLA_LEVELS + 1)).astype(BF16)
                st_ref[h] = st[h] * d_b[c - 1:c, :] + _tn_dot(v_ref[rows[j], vcols[h]], kd)

        for j, h in pairs:
            o = out[j, h]
            ms = jnp.mean(o * o, axis=-1, keepdims=True)
            gate = gate_ref[rows[j], vcols[h]].astype(F32)
            o = o * lax.rsqrt(ms + EPS) * g_head * (gate * _sigmoid(gate))
            o_ref[rows[j], vcols[h]] = o.astype(BF16)
        return carry

    lax.fori_loop(0, chunks_per_block // GLA_GROUP, group, 0)


def _gla(u, log_a, g_head, *, batch, tc):
    n = u.shape[0]
    seq = n // batch
    nblk = seq // tc
    assert seq % tc == 0 and tc % (GLA_CHUNK * GLA_GROUP) == 0, (seq, tc)
    a_mat = jnp.asarray(_gla_decay_matrix(), BF16)
    lvl = jnp.asarray(_gla_level_matrix())
    rows = lambda b, i: b * nblk + i
    return pl.pallas_call(
        functools.partial(_gla_kernel, chunks_per_block=tc // GLA_CHUNK),
        grid=(batch, nblk),
        in_specs=[
            pl.BlockSpec((tc, D_QK), lambda b, i: (rows(b, i), U_GQ // D_QK)),
            pl.BlockSpec((tc, D_QK), lambda b, i: (rows(b, i), U_GK // D_QK)),
            pl.BlockSpec((tc, D_GLA), lambda b, i: (rows(b, i), U_GV // D_GLA)),
            pl.BlockSpec((tc, D_GLA), lambda b, i: (rows(b, i), U_GG // D_GLA)),
            pl.BlockSpec((tc, D_QK), lambda b, i: (rows(b, i), 0)),
            pl.BlockSpec(a_mat.shape, lambda b, i: (0, 0)),
            pl.BlockSpec(lvl.shape, lambda b, i: (0, 0)),
            pl.BlockSpec((1, GLA_DV), lambda b, i: (0, 0)),
        ],
        out_specs=pl.BlockSpec((tc, D_GLA), lambda b, i: (rows(b, i), 0)),
        out_shape=jax.ShapeDtypeStruct((n, D_GLA), BF16),
        scratch_shapes=[pltpu.VMEM((GLA_HEADS, GLA_DV, GLA_DK), F32)],
        compiler_params=pltpu.CompilerParams(
            dimension_semantics=("parallel", "arbitrary"),
            vmem_limit_bytes=VMEM_LIMIT_BYTES),
        name="gla",
    )(u, u, u, u, log_a, a_mat, lvl, g_head)


def _attn_out_kernel(q_ref, k_ref, v_ref, gate_ref, tri_ref, og_ref, x_ref, p_ref,
                     wo_ref, gpost_ref, wg_ref, bg_ref, wp_ref,
                     o_ref, carry_ref, acc_ref, osb_ref, *, tq, nq, tiles):
    t = pl.program_id(0)
    qi = lax.rem(jnp.minimum(t, tiles - 1), nq)
    heads = SB_HEADS
    d = o_ref.shape[1]

    @pl.when(t == 0)
    def _():
        osb_ref[1] = jnp.zeros(osb_ref.shape[1:], BF16)

    tri = tri_ref[...]
    row = lax.broadcasted_iota(jnp.int32, (tq, tq), 0)
    col = lax.broadcasted_iota(jnp.int32, (tq, tq), 1)
    before = col < row
    hcols = [slice(h * SB_DH, (h + 1) * SB_DH) for h in range(heads)]
    qn = [(q_ref[:, c].astype(F32) * (-(SB_DH ** -0.5) * LOG2_E)).astype(BF16) for c in hcols]

    def logits_stage(h, kb, diagonal):
        ks = pl.ds(pl.multiple_of(kb * tq, tq), tq)
        y = _nt_dot(qn[h], k_ref[ks, hcols[h]])
        soft = jnp.log2(1.0 + jnp.exp2(-jnp.abs(y)))
        log_keep = jnp.minimum(y, 0.0) - soft
        if diagonal:
            log_keep = jnp.where(before, log_keep, 0.0)
        return log_keep - y, log_keep.astype(BF16), jnp.sum(log_keep, axis=-1, keepdims=True)

    def weights_stage(log_beta, keep_bf16, carry, diagonal):
        suffix = _dot(keep_bf16, tri)
        if diagonal:
            return jnp.where(before, jnp.exp2(log_beta + suffix), 0.0).astype(BF16)
        return jnp.exp2(log_beta + suffix + carry).astype(BF16)

    def output_stage(h, kb, w):
        ks = pl.ds(pl.multiple_of(kb * tq, tq), tq)
        return _dot(w, v_ref[ks, hcols[h]])

    def run_blocks(blocks, fillers=()):
        n = len(blocks)
        fillers = list(fillers)
        stage_a, stage_b, carries, outs = [None] * n, [None] * n, [None] * n, [None] * n
        for s in range(max(n + 2 * SB_SKEW, len(fillers))):
            if s < n:
                h, kb, diagonal, _, _ = blocks[s]
                stage_a[s] = logits_stage(h, kb, diagonal)
            b = s - SB_SKEW
            if 0 <= b < n:
                h, kb, diagonal, prev, carry = blocks[b]
                if prev is not None:
                    carry = carries[prev] + carry
                log_beta, keep_bf16, row_sum = stage_a[b]
                stage_b[b] = weights_stage(log_beta, keep_bf16, carry, diagonal)
                carries[b] = row_sum if diagonal else carry + row_sum
            c = s - 2 * SB_SKEW
            if 0 <= c < n:
                h, kb, _, _, _ = blocks[c]
                outs[c] = output_stage(h, kb, stage_b[c])
            if s < len(fillers):
                fillers[s]()
        return carries, outs

    nchunk = d // SB_PROJ_CHUNK
    ccols = [slice(c * SB_PROJ_CHUNK, (c + 1) * SB_PROJ_CHUNK) for c in range(nchunk)]
    og = og_ref[...]
    osb = osb_ref[lax.rem(t + 1, 2)]
    state = {"mix": [None] * nchunk}

    def mix_chunk(c):
        def thunk():
            state["mix"][c] = _dot(og, wo_ref[:D_GLA, ccols[c]]) + _dot(osb, wo_ref[D_GLA:, ccols[c]])
        return thunk

    def norm_step():
        mix = jnp.concatenate(state["mix"], axis=1)
        ms = jnp.mean(mix * mix, axis=-1, keepdims=True)
        h1 = x_ref[...] + mix * lax.rsqrt(ms + EPS) * gpost_ref[...]
        state["h1"], state["h1b"], state["pb"] = h1, h1.astype(BF16), p_ref[...].astype(BF16)

    def out_chunk(c):
        def thunk():
            gate = _sigmoid(_dot(state["h1b"], wg_ref[:, ccols[c]]) + bg_ref[:, ccols[c]])
            emb = _dot(state["pb"], wp_ref[:, ccols[c]])
            o_ref[:, ccols[c]] = state["h1"][:, ccols[c]] + gate * emb
        return thunk

    fillers = [mix_chunk(c) for c in range(nchunk)] + [norm_step] + [out_chunk(c) for c in range(nchunk)]

    first = jnp.where(qi == 0, SB_NEG_BIG, 0.0).astype(F32)
    kb_prev = jnp.maximum(qi - 1, 0)
    blocks = []
    for h in range(heads):
        blocks += [(h, qi, True, None, None), (h, kb_prev, False, 2 * h, first)]
    carries, outs = run_blocks(blocks, fillers)
    for h in range(heads):
        carry_ref[h] = carries[2 * h + 1]
        acc_ref[:, hcols[h]] = outs[2 * h] + outs[2 * h + 1]

    def more(state_):
        kb, live = state_
        return jnp.logical_and(kb >= 0, live)

    def body(state_):
        kb, _ = state_
        cs, os_ = run_blocks([(h, kb, False, None, carry_ref[h]) for h in range(heads)])
        top = None
        for h in range(heads):
            carry_ref[h] = cs[h]
            acc_ref[:, hcols[h]] += os_[h]
            top = jnp.max(cs[h]) if top is None else jnp.maximum(top, jnp.max(cs[h]))
        return kb - 1, top > SB_LOG2_ZERO

    lax.while_loop(more, body, (qi - 2, jnp.max(carry_ref[...]) > SB_LOG2_ZERO))
    gate = gate_ref[...].astype(F32)
    osb_ref[lax.rem(t, 2)] = (acc_ref[...] * (gate * _sigmoid(gate))).astype(BF16)


def _attn_out(u, o_gla, x2, p2, w_out, g_post, w_gate, b_gate, w_proj, *, batch, tq):
    n, d = x2.shape
    dp = p2.shape[1]
    dmix = w_out.shape[0]
    seq = n // batch
    nq = seq // tq
    tiles = n // tq
    assert seq % tq == 0 and d % SB_PROJ_CHUNK == 0, (seq, tq, d)
    j = np.arange(tq)[:, None]
    s = np.arange(tq)[None, :]
    tri = jnp.asarray((j > s).astype(np.float32), BF16)
    cur = lambda t: jnp.minimum(t, tiles - 1)
    prv = lambda t: jnp.maximum(t - 1, 0)
    const = lambda t: (0, 0)
    once = pl.Buffered(1)
    return pl.pallas_call(
        functools.partial(_attn_out_kernel, tq=tq, nq=nq, tiles=tiles),
        grid=(tiles + 1,),
        in_specs=[
            pl.BlockSpec((tq, D_SB), lambda t: (cur(t), U_SQ // D_SB)),
            pl.BlockSpec((seq, D_SB), lambda t: (cur(t) // nq, U_SK // D_SB), pipeline_mode=once),
            pl.BlockSpec((seq, D_SB), lambda t: (cur(t) // nq, U_SV // D_SB), pipeline_mode=once),
            pl.BlockSpec((tq, D_SB), lambda t: (cur(t), U_SG // D_SB)),
            pl.BlockSpec((tq, tq), const),
            pl.BlockSpec((tq, D_GLA), lambda t: (prv(t), 0)),
            pl.BlockSpec((tq, d), lambda t: (prv(t), 0)),
            pl.BlockSpec((tq, dp), lambda t: (prv(t), 0)),
            pl.BlockSpec((dmix, d), const, pipeline_mode=once),
            pl.BlockSpec((1, d), const),
            pl.BlockSpec((d, d), const, pipeline_mode=once),
            pl.BlockSpec((1, d), const),
            pl.BlockSpec((dp, d), const, pipeline_mode=once),
        ],
        out_specs=pl.BlockSpec((tq, d), lambda t: (prv(t), 0)),
        out_shape=jax.ShapeDtypeStruct((n, d), F32),
        scratch_shapes=[pltpu.VMEM((SB_HEADS, tq, 1), F32), pltpu.VMEM((tq, D_SB), F32),
                        pltpu.VMEM((2, tq, D_SB), BF16)],
        compiler_params=pltpu.CompilerParams(
            dimension_semantics=("arbitrary",),
            vmem_limit_bytes=VMEM_LIMIT_BYTES),
        name="attn_out",
    )(u, u, u, u, tri, o_gla, x2, p2, w_out, g_post, w_gate, b_gate, w_proj)


def kernel(x, p, g_pre, w_in, w_a2, b_a, g_gla_head, w_out, g_post, w_ple_gate, b_ple_gate, w_ple_proj):
    batch, seq, d = x.shape
    n = batch * seq
    depth = w_in.shape[0]
    lr0 = U_SQ
    h_res = x.astype(F32).reshape(n, d)
    for i in range(depth):
        wt = jnp.transpose(w_in[i])
        wt_main = _w_prep(wt, tc=256)
        wt_lr = jnp.pad(wt[lr0:lr0 + GLA_RANK], ((0, LANES - GLA_RANK), (0, 0))).astype(BF16)
        w_a2p = jnp.pad(w_a2[i], ((0, LANES - GLA_RANK), (0, 0))).astype(BF16)
        u, log_a = _in_proj(h_res, g_pre[i][None, :], wt_main, wt_lr, w_a2p, b_a[i][None, :],
                            tm=1024, tn=1792)
        o_gla = _gla(u, log_a, g_gla_head[i][None, :], batch=batch, tc=1024)
        h_res = _attn_out(u, o_gla, h_res, p[i].astype(F32).reshape(n, -1),
                          w_out[i].astype(BF16), g_post[i][None, :],
                          w_ple_gate[i].astype(BF16), b_ple_gate[i][None, :],
                          w_ple_proj[i].astype(BF16), batch=batch, tq=256)
    return h_res.reshape(batch, seq, d).astype(x.dtype)
```

```python
import functools

import numpy as np
import jax
import jax.numpy as jnp
from jax import lax
from jax.experimental import pallas as pl
from jax.experimental.pallas import tpu as pltpu

F32 = jnp.float32
BF16 = jnp.bfloat16

EPS = 1e-6
GLA_HEADS = 4
GLA_DK = 128
GLA_DV = 256
GLA_RANK = 16
GLA_TAU = 16.0
GLA_CHUNK = 64
GLA_LEVELS = 6
GLA_GROUP = 4
SB_HEADS = 8
SB_DH = 128

LOG2_E = 1.4426950408889634
SB_LOG2_ZERO = -160.0
SB_SKEW = 1
SB_NEG_BIG = -1e30
SB_PROJ_CHUNK = 256

LANES = 128
VMEM_LIMIT_BYTES = 60 * 1024 * 1024

D_QK = GLA_HEADS * GLA_DK
D_GLA = GLA_HEADS * GLA_DV
D_SB = SB_HEADS * SB_DH
U_GQ, U_GK, U_GV, U_GG = 0, D_QK, 2 * D_QK, 2 * D_QK + D_GLA
U_SQ = U_GG + D_GLA
U_SK, U_SV, U_SG = U_SQ + D_SB, U_SQ + 2 * D_SB, U_SQ + 3 * D_SB
D_U = U_SG + D_SB


def _nt_dot(a, b):
    return lax.dot_general(a, b, (((1,), (1,)), ((), ())), preferred_element_type=F32)


def _tn_dot(a, b):
    return lax.dot_general(a, b, (((0,), (0,)), ((), ())), preferred_element_type=F32)


def _dot(a, b):
    return jnp.dot(a, b, preferred_element_type=F32)


def _split_bf16(x):
    hi = x.astype(BF16)
    lo = (x - hi.astype(F32)).astype(BF16)
    return hi, lo


def _sigmoid(x):
    return 1.0 / (1.0 + jnp.exp(-x))


def _log1pexp_neg_abs(z):
    return jnp.log(1.0 + jnp.exp(-jnp.abs(z)))


def _w_prep_kernel(wt_ref, o_ref):
    o_ref[:U_SQ, :] = wt_ref[:U_SQ, :].astype(BF16)
    o_ref[U_SQ:, :] = wt_ref[U_SQ + GLA_RANK:, :].astype(BF16)


def _w_prep(wt, *, tc):
    d_in, d = wt.shape
    return pl.pallas_call(
        _w_prep_kernel,
        grid=(d // tc,),
        in_specs=[pl.BlockSpec((d_in, tc), lambda r: (0, r))],
        out_specs=pl.BlockSpec((D_U, tc), lambda r: (0, r)),
        out_shape=jax.ShapeDtypeStruct((D_U, d), BF16),
        compiler_params=pltpu.CompilerParams(
            dimension_semantics=("parallel",),
            vmem_limit_bytes=VMEM_LIMIT_BYTES),
        name="w_prep",
    )(wt)


def _in_proj_kernel(x_ref, g_ref, w_ref, wlr_ref, wa2_ref, ba_ref, u_ref, la_ref, h_scr, glr_scr,
                    *, col_steps):
    j = pl.program_id(1)

    @pl.when(j == 0)
    def _():
        x = x_ref[...]
        ms = jnp.mean(x * x, axis=-1, keepdims=True)
        hb = (x * lax.rsqrt(ms + EPS) * g_ref[...]).astype(BF16)
        h_scr[...] = hb
        glr_scr[...] = _nt_dot(hb, wlr_ref[...]).astype(BF16)

    tr = la_ref.shape[0] // col_steps
    rows = pl.ds(pl.multiple_of(j * tr, tr), tr)
    z = _dot(glr_scr[rows, :], wa2_ref[...]) + ba_ref[...]
    la_ref[rows, :] = (jnp.minimum(z, 0.0) - _log1pexp_neg_abs(z)) * (LOG2_E / GLA_TAU)

    u_ref[...] = _nt_dot(h_scr[...], w_ref[...]).astype(BF16)


def _in_proj(x2, g_pre, wt_main, wt_lr, w_a2, b_a, *, tm, tn):
    n, d = x2.shape
    du = wt_main.shape[0]
    col_steps = du // tn
    assert du % tn == 0 and n % tm == 0 and tm % (16 * col_steps) == 0, (n, du, tm, tn)
    return pl.pallas_call(
        functools.partial(_in_proj_kernel, col_steps=col_steps),
        grid=(n // tm, col_steps),
        in_specs=[
            pl.BlockSpec((tm, d), lambda i, j: (i, 0)),
            pl.BlockSpec((1, d), lambda i, j: (0, 0)),
            pl.BlockSpec((tn, d), lambda i, j: (j, 0)),
            pl.BlockSpec((LANES, d), lambda i, j: (0, 0)),
            pl.BlockSpec((LANES, D_QK), lambda i, j: (0, 0)),
            pl.BlockSpec((1, D_QK), lambda i, j: (0, 0)),
        ],
        out_specs=[
            pl.BlockSpec((tm, tn), lambda i, j: (i, j)),
            pl.BlockSpec((tm, D_QK), lambda i, j: (i, 0)),
        ],
        out_shape=[
            jax.ShapeDtypeStruct((n, du), BF16),
            jax.ShapeDtypeStruct((n, D_QK), F32),
        ],
        scratch_shapes=[pltpu.VMEM((tm, d), BF16), pltpu.VMEM((tm, LANES), BF16)],
        compiler_params=pltpu.CompilerParams(
            dimension_semantics=("parallel", "arbitrary"),
            vmem_limit_bytes=VMEM_LIMIT_BYTES),
        name="in_proj",
    )(x2, g_pre, wt_main, wt_lr, w_a2, b_a)


def _gla_decay_matrix():
    c = GLA_CHUNK
    r = np.arange(c)[:, None]
    t = np.arange(c)[None, :]
    groups = []
    for lvl in range(GLA_LEVELS):
        s = 1 << lvl
        start = (r // (2 * s)) * (2 * s)
        m = start + s - 1
        upper = (r >= start + s) & (t > m) & (t <= r)
        lower = (r <= m) & (t > r) & (t <= m)
        groups.append(upper | lower)
    groups += [t <= r, t > r]
    a = np.concatenate(groups, axis=0).astype(np.float32)
    return np.concatenate([a, a], axis=1)


def _gla_level_matrix():
    c = GLA_CHUNK
    i = np.arange(c)[:, None]
    j = np.arange(c)[None, :]
    lvl = np.full((c, c), GLA_LEVELS + 1, np.int32)
    lvl[i == j] = GLA_LEVELS
    x = i ^ j
    for b in range(GLA_LEVELS):
        lvl[(i > j) & ((x >> b) == 1)] = b
    return lvl


def _gla_thunks(q_ref, k_ref, v_ref, gate_ref, la_ref, a_mat, lvl, g_head, st_ref, store):
    c = GLA_CHUNK
    rows = [slice(j * c, (j + 1) * c) for j in range(GLA_GROUP)]
    kcols = [slice(h * GLA_DK, (h + 1) * GLA_DK) for h in range(GLA_HEADS)]
    vcols = [slice(h * GLA_DV, (h + 1) * GLA_DV) for h in range(GLA_HEADS)]
    pairs = [(j, h) for j in range(GLA_GROUP) for h in range(GLA_HEADS)]
    scale = GLA_DK ** -0.5
    decay, q, k, scores, intra, out = {}, {}, {}, {}, {}, {}

    def dec(j, h, g):
        return decay[j][g * c:(g + 1) * c, kcols[h]]

    def decay_thunk(j):
        def thunk():
            la_parts = jnp.concatenate(_split_bf16(la_ref[rows[j], :]), axis=0)
            decay[j] = jnp.exp2(_dot(a_mat, la_parts))
            for h in range(GLA_HEADS):
                q[j, h] = q_ref[rows[j], kcols[h]].astype(F32) * scale
                k[j, h] = k_ref[rows[j], kcols[h]].astype(F32)
        return thunk

    def diag_thunk():
        for p in pairs:
            scores[p] = jnp.where(lvl == GLA_LEVELS, _nt_dot(q[p].astype(BF16), k[p].astype(BF16)), 0.0)

    def level_thunk(l):
        def thunk():
            for j, h in pairs:
                ql = (q[j, h] * dec(j, h, l)).astype(BF16)
                kl = (k[j, h] * dec(j, h, l)).astype(BF16)
                scores[j, h] = jnp.where(lvl == l, _nt_dot(ql, kl), scores[j, h])
        return thunk

    def intra_thunk():
        for j, h in pairs:
            intra[j, h] = _dot(scores[j, h].astype(BF16), v_ref[rows[j], vcols[h]])

    def state_thunk(j):
        def thunk():
            st = [st_ref[h] for h in range(GLA_HEADS)]
            for h in range(GLA_HEADS):
                d_b = dec(j, h, GLA_LEVELS)
                out[j, h] = intra[j, h] + _nt_dot((q[j, h] * d_b).astype(BF16), st[h].astype(BF16))
            for h in range(GLA_HEADS):
                d_b = dec(j, h, GLA_LEVELS)
                kd = (k[j, h] * dec(j, h, GLA_LEVELS + 1)).astype(BF16)
                st_ref[h] = st[h] * d_b[c - 1:c, :] + _tn_dot(v_ref[rows[j], vcols[h]], kd)
        return thunk

    def finish_thunk(j):
        def thunk():
            for h in range(GLA_HEADS):
                o = out[j, h]
                ms = jnp.mean(o * o, axis=-1, keepdims=True)
                gate = gate_ref[rows[j], vcols[h]].astype(F32)
                o = o * lax.rsqrt(ms + EPS) * g_head * (gate * _sigmoid(gate))
                store(rows[j], vcols[h], o.astype(BF16))
        return thunk

    return ([decay_thunk(j) for j in range(GLA_GROUP)] + [diag_thunk]
            + [level_thunk(l) for l in range(GLA_LEVELS)] + [intra_thunk]
            + [state_thunk(j) for j in range(GLA_GROUP)] + [finish_thunk(j) for j in range(GLA_GROUP)])


def _tile_kernel(sq_ref, kc_ref, vc_ref, kp_ref, vp_ref, sgate_ref, tri_ref,
                 gq_ref, gk_ref, gv_ref, ggate_ref, la_ref, a_ref, lvl_ref, gh_ref,
                 x_ref, p_ref, wo_ref, gpost_ref, wg_ref, bg_ref, wp_ref, u_any,
                 o_ref, carry_ref, acc_ref, mix_ref, st_ref, kv_ref, sem, *, tq, nq, tiles):
    t = pl.program_id(0)
    cur = jnp.minimum(t, tiles - 1)
    qi = lax.rem(cur, nq)
    seq_row0 = (cur - qi) * tq
    slot = lax.rem(t, 2)
    heads = SB_HEADS
    d = o_ref.shape[1]

    @pl.when(t == 0)
    def _():
        mix_ref[1] = jnp.zeros(mix_ref.shape[1:], BF16)

    @pl.when(qi == 0)
    def _():
        st_ref[...] = jnp.zeros_like(st_ref)

    tri = tri_ref[...]
    row = lax.broadcasted_iota(jnp.int32, (tq, tq), 0)
    col = lax.broadcasted_iota(jnp.int32, (tq, tq), 1)
    before = col < row
    hcols = [slice(h * SB_DH, (h + 1) * SB_DH) for h in range(heads)]
    qn = [(sq_ref[:, c].astype(F32) * (-(SB_DH ** -0.5) * LOG2_E)).astype(BF16) for c in hcols]

    def logits_stage(h, keys, diagonal):
        y = _nt_dot(qn[h], keys[:, hcols[h]])
        soft = jnp.log2(1.0 + jnp.exp2(-jnp.abs(y)))
        log_keep = jnp.minimum(y, 0.0) - soft
        if diagonal:
            log_keep = jnp.where(before, log_keep, 0.0)
        return log_keep - y, log_keep.astype(BF16), jnp.sum(log_keep, axis=-1, keepdims=True)

    def weights_stage(log_beta, keep_bf16, carry, diagonal):
        suffix = _dot(keep_bf16, tri)
        if diagonal:
            return jnp.where(before, jnp.exp2(log_beta + suffix), 0.0).astype(BF16)
        return jnp.exp2(log_beta + suffix + carry).astype(BF16)

    def run_blocks(blocks, fillers=()):
        n = len(blocks)
        fillers = list(fillers)
        stage_a, stage_b, carries, outs = [None] * n, [None] * n, [None] * n, [None] * n
        for s in range(max(n + 2 * SB_SKEW, len(fillers))):
            if s < n:
                h, keys, _, diagonal, _, _ = blocks[s]
                stage_a[s] = logits_stage(h, keys, diagonal)
            b = s - SB_SKEW
            if 0 <= b < n:
                h, _, _, diagonal, prev, carry = blocks[b]
                if prev is not None:
                    carry = carries[prev] + carry
                log_beta, keep_bf16, row_sum = stage_a[b]
                stage_b[b] = weights_stage(log_beta, keep_bf16, carry, diagonal)
                carries[b] = row_sum if diagonal else carry + row_sum
            c = s - 2 * SB_SKEW
            if 0 <= c < n:
                h, _, values, _, _, _ = blocks[c]
                outs[c] = _dot(stage_b[c], values[:, hcols[h]])
            if s < len(fillers):
                fillers[s]()
        return carries, outs

    nchunk = d // SB_PROJ_CHUNK
    ccols = [slice(c * SB_PROJ_CHUNK, (c + 1) * SB_PROJ_CHUNK) for c in range(nchunk)]
    mixer = mix_ref[1 - slot]
    state = {"mix": [None] * nchunk}

    def mix_chunk(c):
        def thunk():
            state["mix"][c] = _dot(mixer, wo_ref[:, ccols[c]])
        return thunk

    def norm_step():
        mix = jnp.concatenate(state["mix"], axis=1)
        ms = jnp.mean(mix * mix, axis=-1, keepdims=True)
        h1 = x_ref[...] + mix * lax.rsqrt(ms + EPS) * gpost_ref[...]
        state["h1"], state["h1b"], state["pb"] = h1, h1.astype(BF16), p_ref[...].astype(BF16)

    def out_chunk(c):
        def thunk():
            gate = _sigmoid(_dot(state["h1b"], wg_ref[:, ccols[c]]) + bg_ref[:, ccols[c]])
            emb = _dot(state["pb"], wp_ref[:, ccols[c]])
            o_ref[:, ccols[c]] = state["h1"][:, ccols[c]] + gate * emb
        return thunk

    proj = [mix_chunk(c) for c in range(nchunk)] + [norm_step] + [out_chunk(c) for c in range(nchunk)]

    def store_gla(rows, cols, value):
        mix_ref[slot, rows, cols] = value

    gla = _gla_thunks(gq_ref, gk_ref, gv_ref, ggate_ref, la_ref, a_ref[...], lvl_ref[...], gh_ref[...],
                      st_ref, store_gla)

    def both(i):
        def thunk():
            if i < len(proj):
                proj[i]()
            if i < len(gla):
                gla[i]()
        return thunk

    fillers = [both(i) for i in range(max(len(proj), len(gla)))]

    first = jnp.where(qi == 0, SB_NEG_BIG, 0.0).astype(F32)
    blocks = []
    for h in range(heads):
        blocks += [(h, kc_ref, vc_ref, True, None, None), (h, kp_ref, vp_ref, False, 2 * h, first)]
    carries, outs = run_blocks(blocks, fillers)
    for h in range(heads):
        carry_ref[h] = carries[2 * h + 1]
        acc_ref[:, hcols[h]] = outs[2 * h] + outs[2 * h + 1]

    def more(state_):
        kb, live = state_
        return jnp.logical_and(kb >= 0, live)

    def body(state_):
        kb, _ = state_
        r0 = pl.multiple_of(seq_row0 + kb * tq, tq)
        copies = [pltpu.make_async_copy(u_any.at[pl.ds(r0, tq), pl.ds(off, D_SB)], kv_ref.at[i], sem.at[i])
                  for i, off in enumerate((U_SK, U_SV))]
        for cp in copies:
            cp.start()
        for cp in copies:
            cp.wait()
        cs, os_ = run_blocks([(h, kv_ref.at[0], kv_ref.at[1], False, None, carry_ref[h]) for h in range(heads)])
        top = None
        for h in range(heads):
            carry_ref[h] = cs[h]
            acc_ref[:, hcols[h]] += os_[h]
            top = jnp.max(cs[h]) if top is None else jnp.maximum(top, jnp.max(cs[h]))
        return kb - 1, top > SB_LOG2_ZERO

    lax.while_loop(more, body, (qi - 2, jnp.max(carry_ref[...]) > SB_LOG2_ZERO))
    gate = sgate_ref[...].astype(F32)
    mix_ref[slot, :, D_GLA:] = (acc_ref[...] * (gate * _sigmoid(gate))).astype(BF16)


def _tile_call(u, log_a, g_head, x2, p2, w_out, g_post, w_gate, b_gate, w_proj, *, batch, tq):
    n, d = x2.shape
    dp = p2.shape[1]
    dmix = w_out.shape[0]
    seq = n // batch
    nq = seq // tq
    tiles = n // tq
    assert seq % tq == 0 and d % SB_PROJ_CHUNK == 0 and tq == GLA_CHUNK * GLA_GROUP, (seq, tq, d)
    assert dmix == D_GLA + D_SB, dmix
    j = np.arange(tq)[:, None]
    s = np.arange(tq)[None, :]
    tri = jnp.asarray((j > s).astype(np.float32), BF16)
    a_mat = jnp.asarray(_gla_decay_matrix(), BF16)
    lvl = jnp.asarray(_gla_level_matrix())
    cur = lambda t: jnp.minimum(t, tiles - 1)
    bef = lambda t: jnp.maximum(cur(t) - 1, 0)
    prv = lambda t: jnp.maximum(t - 1, 0)
    const = lambda t: (0, 0)
    once = pl.Buffered(1)
    ucol = lambda off, width: off // width
    return pl.pallas_call(
        functools.partial(_tile_kernel, tq=tq, nq=nq, tiles=tiles),
        grid=(tiles + 1,),
        in_specs=[
            pl.BlockSpec((tq, D_SB), lambda t: (cur(t), ucol(U_SQ, D_SB))),
            pl.BlockSpec((tq, D_SB), lambda t: (cur(t), ucol(U_SK, D_SB))),
            pl.BlockSpec((tq, D_SB), lambda t: (cur(t), ucol(U_SV, D_SB))),
            pl.BlockSpec((tq, D_SB), lambda t: (bef(t), ucol(U_SK, D_SB))),
            pl.BlockSpec((tq, D_SB), lambda t: (bef(t), ucol(U_SV, D_SB))),
            pl.BlockSpec((tq, D_SB), lambda t: (cur(t), ucol(U_SG, D_SB))),
            pl.BlockSpec((tq, tq), const),
            pl.BlockSpec((tq, D_QK), lambda t: (cur(t), ucol(U_GQ, D_QK))),
            pl.BlockSpec((tq, D_QK), lambda t: (cur(t), ucol(U_GK, D_QK))),
            pl.BlockSpec((tq, D_GLA), lambda t: (cur(t), ucol(U_GV, D_GLA))),
            pl.BlockSpec((tq, D_GLA), lambda t: (cur(t), ucol(U_GG, D_GLA))),
            pl.BlockSpec((tq, D_QK), lambda t: (cur(t), 0)),
            pl.BlockSpec(a_mat.shape, const),
            pl.BlockSpec(lvl.shape, const),
            pl.BlockSpec((1, GLA_DV), const),
            pl.BlockSpec((tq, d), lambda t: (prv(t), 0)),
            pl.BlockSpec((tq, dp), lambda t: (prv(t), 0)),
            pl.BlockSpec((dmix, d), const, pipeline_mode=once),
            pl.BlockSpec((1, d), const),
            pl.BlockSpec((d, d), const, pipeline_mode=once),
            pl.BlockSpec((1, d), const),
            pl.BlockSpec((dp, d), const, pipeline_mode=once),
            pl.BlockSpec(memory_space=pl.ANY),
        ],
        out_specs=pl.BlockSpec((tq, d), lambda t: (prv(t), 0)),
        out_shape=jax.ShapeDtypeStruct((n, d), F32),
        scratch_shapes=[
            pltpu.VMEM((SB_HEADS, tq, 1), F32),
            pltpu.VMEM((tq, D_SB), F32),
            pltpu.VMEM((2, tq, D_GLA + D_SB), BF16),
            pltpu.VMEM((GLA_HEADS, GLA_DV, GLA_DK), F32),
            pltpu.VMEM((2, tq, D_SB), BF16),
            pltpu.SemaphoreType.DMA((2,)),
        ],
        compiler_params=pltpu.CompilerParams(
            dimension_semantics=("arbitrary",),
            vmem_limit_bytes=VMEM_LIMIT_BYTES),
        name="mixer_out",
    )(u, u, u, u, u, u, tri, u, u, u, u, log_a, a_mat, lvl, g_head,
      x2, p2, w_out, g_post, w_gate, b_gate, w_proj, u)


def kernel(x, p, g_pre, w_in, w_a2, b_a, g_gla_head, w_out, g_post, w_ple_gate, b_ple_gate, w_ple_proj):
    batch, seq, d = x.shape
    n = batch * seq
    depth = w_in.shape[0]
    lr0 = U_SQ
    h_res = x.astype(F32).reshape(n, d)
    for i in range(depth):
        wt = jnp.transpose(w_in[i])
        wt_main = _w_prep(wt, tc=256)
        wt_lr = jnp.pad(wt[lr0:lr0 + GLA_RANK], ((0, LANES - GLA_RANK), (0, 0))).astype(BF16)
        w_a2p = jnp.pad(w_a2[i], ((0, LANES - GLA_RANK), (0, 0))).astype(BF16)
        u, log_a = _in_proj(h_res, g_pre[i][None, :], wt_main, wt_lr, w_a2p, b_a[i][None, :],
                            tm=1024, tn=1792)
        h_res = _tile_call(u, log_a, g_gla_head[i][None, :], h_res, p[i].astype(F32).reshape(n, -1),
                           w_out[i].astype(BF16), g_post[i][None, :],
                           w_ple_gate[i].astype(BF16), b_ple_gate[i][None, :],
                           w_ple_proj[i].astype(BF16), batch=batch, tq=256)
    return h_res.reshape(batch, seq, d).astype(x.dtype)
```

```python
import functools

import numpy as np
import jax
import jax.numpy as jnp
from jax import lax
from jax.experimental import pallas as pl
from jax.experimental.pallas import tpu as pltpu

F32 = jnp.float32
BF16 = jnp.bfloat16

EPS = 1e-6
GLA_HEADS = 4
GLA_DK = 128
GLA_DV = 256
GLA_RANK = 16
GLA_TAU = 16.0
GLA_CHUNK = 64
GLA_LEVELS = 6
GLA_GROUP = 4
SB_HEADS = 8
SB_DH = 128

LOG2_E = 1.4426950408889634
SB_LOG2_ZERO = -160.0
SB_SKEW = 1
SB_NEG_BIG = -1e30
SB_PROJ_CHUNK = 256

LANES = 128
VMEM_LIMIT_BYTES = 60 * 1024 * 1024

D_QK = GLA_HEADS * GLA_DK
D_GLA = GLA_HEADS * GLA_DV
D_SB = SB_HEADS * SB_DH
U_GQ, U_GK, U_GV, U_GG = 0, D_QK, 2 * D_QK, 2 * D_QK + D_GLA
U_SQ = U_GG + D_GLA
U_SK, U_SV, U_SG = U_SQ + D_SB, U_SQ + 2 * D_SB, U_SQ + 3 * D_SB
D_U = U_SG + D_SB


def _nt_dot(a, b):
    return lax.dot_general(a, b, (((1,), (1,)), ((), ())), preferred_element_type=F32)


def _tn_dot(a, b):
    return lax.dot_general(a, b, (((0,), (0,)), ((), ())), preferred_element_type=F32)


def _dot(a, b):
    return jnp.dot(a, b, preferred_element_type=F32)


def _split_bf16(x):
    hi = x.astype(BF16)
    lo = (x - hi.astype(F32)).astype(BF16)
    return hi, lo


def _sigmoid(x):
    return 1.0 / (1.0 + jnp.exp(-x))


def _log1pexp_neg_abs(z):
    return jnp.log(1.0 + jnp.exp(-jnp.abs(z)))


def _w_prep_kernel(wt_ref, o_ref):
    o_ref[:U_SQ, :] = wt_ref[:U_SQ, :].astype(BF16)
    o_ref[U_SQ:, :] = wt_ref[U_SQ + GLA_RANK:, :].astype(BF16)


def _w_prep(wt, *, tc):
    d_in, d = wt.shape
    return pl.pallas_call(
        _w_prep_kernel,
        grid=(d // tc,),
        in_specs=[pl.BlockSpec((d_in, tc), lambda r: (0, r))],
        out_specs=pl.BlockSpec((D_U, tc), lambda r: (0, r)),
        out_shape=jax.ShapeDtypeStruct((D_U, d), BF16),
        compiler_params=pltpu.CompilerParams(
            dimension_semantics=("parallel",),
            vmem_limit_bytes=VMEM_LIMIT_BYTES),
        name="w_prep",
    )(wt)


def _in_proj_kernel(x_ref, g_ref, wlr_ref, wa2_ref, ba_ref, w_any, u_any, la_ref,
                    h_scr, w_buf, u_buf, w_sem, u_sem, *, col_steps, tn):
    i = pl.program_id(0)
    last = pl.num_programs(0) - 1
    tm = h_scr.shape[0]

    def w_copy(j):
        return pltpu.make_async_copy(w_any.at[pl.ds(j * tn, tn), :], w_buf.at[j % 2], w_sem.at[j % 2])

    def u_copy(j):
        rows = pl.ds(pl.multiple_of(i * tm, tm), tm)
        return pltpu.make_async_copy(u_buf.at[j % 2], u_any.at[rows, pl.ds(j * tn, tn)], u_sem.at[j % 2])

    @pl.when(i == 0)
    def _():
        w_copy(0).start()

    x = x_ref[...]
    ms = jnp.mean(x * x, axis=-1, keepdims=True)
    hb = (x * lax.rsqrt(ms + EPS) * g_ref[...]).astype(BF16)
    h_scr[...] = hb
    g_lr = _nt_dot(hb, wlr_ref[...]).astype(BF16)

    tr = tm // col_steps
    for j in range(col_steps):
        w_copy(j).wait()
        w_copy((j + 1) % col_steps).start()
        if j >= 2:
            u_copy(j - 2).wait()
        rows = slice(j * tr, (j + 1) * tr)
        z = _dot(g_lr[rows, :], wa2_ref[...]) + ba_ref[...]
        la_ref[rows, :] = (jnp.minimum(z, 0.0) - _log1pexp_neg_abs(z)) * (LOG2_E / GLA_TAU)
        u_buf[j % 2] = _nt_dot(h_scr[...], w_buf[j % 2]).astype(BF16)
        u_copy(j).start()
    u_copy(col_steps - 2).wait()
    u_copy(col_steps - 1).wait()

    @pl.when(i == last)
    def _():
        w_copy(0).wait()


def _in_proj(x2, g_pre, wt_main, wt_lr, w_a2, b_a, *, tm, tn):
    n, d = x2.shape
    du = wt_main.shape[0]
    col_steps = du // tn
    assert du % tn == 0 and n % tm == 0 and col_steps >= 2 and col_steps % 2 == 0, (n, du, tm, tn)
    return pl.pallas_call(
        functools.partial(_in_proj_kernel, col_steps=col_steps, tn=tn),
        grid=(n // tm,),
        in_specs=[
            pl.BlockSpec((tm, d), lambda i: (i, 0)),
            pl.BlockSpec((1, d), lambda i: (0, 0)),
            pl.BlockSpec((LANES, d), lambda i: (0, 0)),
            pl.BlockSpec((LANES, D_QK), lambda i: (0, 0)),
            pl.BlockSpec((1, D_QK), lambda i: (0, 0)),
            pl.BlockSpec(memory_space=pl.ANY),
        ],
        out_specs=[
            pl.BlockSpec(memory_space=pl.ANY),
            pl.BlockSpec((tm, D_QK), lambda i: (i, 0)),
        ],
        out_shape=[
            jax.ShapeDtypeStruct((n, du), BF16),
            jax.ShapeDtypeStruct((n, D_QK), F32),
        ],
        scratch_shapes=[
            pltpu.VMEM((tm, d), BF16),
            pltpu.VMEM((2, tn, d), BF16),
            pltpu.VMEM((2, tm, tn), BF16),
            pltpu.SemaphoreType.DMA((2,)),
            pltpu.SemaphoreType.DMA((2,)),
        ],
        compiler_params=pltpu.CompilerParams(
            dimension_semantics=("arbitrary",),
            vmem_limit_bytes=VMEM_LIMIT_BYTES),
        name="in_proj",
    )(x2, g_pre, wt_lr, w_a2, b_a, wt_main)


def _gla_decay_matrix():
    c = GLA_CHUNK
    r = np.arange(c)[:, None]
    t = np.arange(c)[None, :]
    groups = []
    for lvl in range(GLA_LEVELS):
        s = 1 << lvl
        start = (r // (2 * s)) * (2 * s)
        m = start + s - 1
        upper = (r >= start + s) & (t > m) & (t <= r)
        lower = (r <= m) & (t > r) & (t <= m)
        groups.append(upper | lower)
    groups += [t <= r, t > r]
    a = np.concatenate(groups, axis=0).astype(np.float32)
    return np.concatenate([a, a], axis=1)


def _gla_level_matrix():
    c = GLA_CHUNK
    i = np.arange(c)[:, None]
    j = np.arange(c)[None, :]
    lvl = np.full((c, c), GLA_LEVELS + 1, np.int32)
    lvl[i == j] = GLA_LEVELS
    x = i ^ j
    for b in range(GLA_LEVELS):
        lvl[(i > j) & ((x >> b) == 1)] = b
    return lvl


def _gla_thunks(q_ref, k_ref, v_ref, gate_ref, la_ref, a_mat, lvl, g_head, st_ref, store):
    c = GLA_CHUNK
    rows = [slice(j * c, (j + 1) * c) for j in range(GLA_GROUP)]
    kcols = [slice(h * GLA_DK, (h + 1) * GLA_DK) for h in range(GLA_HEADS)]
    vcols = [slice(h * GLA_DV, (h + 1) * GLA_DV) for h in range(GLA_HEADS)]
    pairs = [(j, h) for j in range(GLA_GROUP) for h in range(GLA_HEADS)]
    scale = GLA_DK ** -0.5
    decay, q, k, scores, intra, out = {}, {}, {}, {}, {}, {}

    def dec(j, h, g):
        return decay[j][g * c:(g + 1) * c, kcols[h]]

    def decay_thunk(j):
        def thunk():
            la_parts = jnp.concatenate(_split_bf16(la_ref[rows[j], :]), axis=0)
            decay[j] = jnp.exp2(_dot(a_mat, la_parts))
            for h in range(GLA_HEADS):
                q[j, h] = q_ref[rows[j], kcols[h]].astype(F32) * scale
                k[j, h] = k_ref[rows[j], kcols[h]].astype(F32)
        return thunk

    def diag_thunk():
        for p in pairs:
            scores[p] = jnp.where(lvl == GLA_LEVELS, _nt_dot(q[p].astype(BF16), k[p].astype(BF16)), 0.0)

    def level_thunk(l):
        def thunk():
            for j, h in pairs:
                ql = (q[j, h] * dec(j, h, l)).astype(BF16)
                kl = (k[j, h] * dec(j, h, l)).astype(BF16)
                scores[j, h] = jnp.where(lvl == l, _nt_dot(ql, kl), scores[j, h])
        return thunk

    def intra_thunk():
        for j, h in pairs:
            intra[j, h] = _dot(scores[j, h].astype(BF16), v_ref[rows[j], vcols[h]])

    def state_thunk(j):
        def thunk():
            st = [st_ref[h] for h in range(GLA_HEADS)]
            for h in range(GLA_HEADS):
                d_b = dec(j, h, GLA_LEVELS)
                out[j, h] = intra[j, h] + _nt_dot((q[j, h] * d_b).astype(BF16), st[h].astype(BF16))
            for h in range(GLA_HEADS):
                d_b = dec(j, h, GLA_LEVELS)
                kd = (k[j, h] * dec(j, h, GLA_LEVELS + 1)).astype(BF16)
                st_ref[h] = st[h] * d_b[c - 1:c, :] + _tn_dot(v_ref[rows[j], vcols[h]], kd)
        return thunk

    def finish_thunk(j):
        def thunk():
            for h in range(GLA_HEADS):
                o = out[j, h]
                ms = jnp.mean(o * o, axis=-1, keepdims=True)
                gate = gate_ref[rows[j], vcols[h]].astype(F32)
                o = o * lax.rsqrt(ms + EPS) * g_head * (gate * _sigmoid(gate))
                store(rows[j], vcols[h], o.astype(BF16))
        return thunk

    return ([decay_thunk(j) for j in range(GLA_GROUP)] + [diag_thunk]
            + [level_thunk(l) for l in range(GLA_LEVELS)] + [intra_thunk]
            + [state_thunk(j) for j in range(GLA_GROUP)] + [finish_thunk(j) for j in range(GLA_GROUP)])


def _tile_kernel(sq_ref, kc_ref, vc_ref, kp_ref, vp_ref, sgate_ref, tri_ref,
                 gq_ref, gk_ref, gv_ref, ggate_ref, la_ref, a_ref, lvl_ref, gh_ref,
                 x_ref, p_ref, wo_ref, gpost_ref, wg_ref, bg_ref, wp_ref, u_any,
                 o_ref, carry_ref, acc_ref, mix_ref, st_ref, kv_ref, sem, *, tq, nq, tiles):
    t = pl.program_id(0)
    cur = jnp.minimum(t, tiles - 1)
    qi = lax.rem(cur, nq)
    seq_row0 = (cur - qi) * tq
    slot = lax.rem(t, 2)
    heads = SB_HEADS
    d = o_ref.shape[1]

    @pl.when(t == 0)
    def _():
        mix_ref[1] = jnp.zeros(mix_ref.shape[1:], BF16)

    @pl.when(qi == 0)
    def _():
        st_ref[...] = jnp.zeros_like(st_ref)

    tri = tri_ref[...]
    row = lax.broadcasted_iota(jnp.int32, (tq, tq), 0)
    col = lax.broadcasted_iota(jnp.int32, (tq, tq), 1)
    before = col < row
    hcols = [slice(h * SB_DH, (h + 1) * SB_DH) for h in range(heads)]
    qn = [(sq_ref[:, c].astype(F32) * (-(SB_DH ** -0.5) * LOG2_E)).astype(BF16) for c in hcols]

    def logits_stage(h, keys, diagonal):
        y = _nt_dot(qn[h], keys[:, hcols[h]])
        soft = jnp.log2(1.0 + jnp.exp2(-jnp.abs(y)))
        log_keep = jnp.minimum(y, 0.0) - soft
        if diagonal:
            log_keep = jnp.where(before, log_keep, 0.0)
        return log_keep - y, log_keep.astype(BF16), jnp.sum(log_keep, axis=-1, keepdims=True)

    def weights_stage(log_beta, keep_bf16, carry, diagonal):
        suffix = _dot(keep_bf16, tri)
        if diagonal:
            return jnp.where(before, jnp.exp2(log_beta + suffix), 0.0).astype(BF16)
        return jnp.exp2(log_beta + suffix + carry).astype(BF16)

    def run_blocks(blocks, fillers=()):
        n = len(blocks)
        fillers = list(fillers)
        stage_a, stage_b, carries, outs = [None] * n, [None] * n, [None] * n, [None] * n
        for s in range(max(n + 2 * SB_SKEW, len(fillers))):
            if s < n:
                h, keys, _, diagonal, _, _ = blocks[s]
                stage_a[s] = logits_stage(h, keys, diagonal)
            b = s - SB_SKEW
            if 0 <= b < n:
                h, _, _, diagonal, prev, carry = blocks[b]
                if prev is not None:
                    carry = carries[prev] + carry
                log_beta, keep_bf16, row_sum = stage_a[b]
                stage_b[b] = weights_stage(log_beta, keep_bf16, carry, diagonal)
                carries[b] = row_sum if diagonal else carry + row_sum
            c = s - 2 * SB_SKEW
            if 0 <= c < n:
                h, _, values, _, _, _ = blocks[c]
                outs[c] = _dot(stage_b[c], values[:, hcols[h]])
            if s < len(fillers):
                fillers[s]()
        return carries, outs

    nchunk = d // SB_PROJ_CHUNK
    ccols = [slice(c * SB_PROJ_CHUNK, (c + 1) * SB_PROJ_CHUNK) for c in range(nchunk)]
    mixer = mix_ref[1 - slot]
    state = {"mix": [None] * nchunk}

    def mix_chunk(c):
        def thunk():
            state["mix"][c] = _dot(mixer, wo_ref[:, ccols[c]])
        return thunk

    def norm_step():
        mix = jnp.concatenate(state["mix"], axis=1)
        ms = jnp.mean(mix * mix, axis=-1, keepdims=True)
        h1 = x_ref[...] + mix * lax.rsqrt(ms + EPS) * gpost_ref[...]
        state["h1"], state["h1b"], state["pb"] = h1, h1.astype(BF16), p_ref[...].astype(BF16)

    def out_chunk(c):
        def thunk():
            gate = _sigmoid(_dot(state["h1b"], wg_ref[:, ccols[c]]) + bg_ref[:, ccols[c]])
            emb = _dot(state["pb"], wp_ref[:, ccols[c]])
            o_ref[:, ccols[c]] = state["h1"][:, ccols[c]] + gate * emb
        return thunk

    proj = [mix_chunk(c) for c in range(nchunk)] + [norm_step] + [out_chunk(c) for c in range(nchunk)]

    def store_gla(rows, cols, value):
        mix_ref[slot, rows, cols] = value

    gla = _gla_thunks(gq_ref, gk_ref, gv_ref, ggate_ref, la_ref, a_ref[...], lvl_ref[...], gh_ref[...],
                      st_ref, store_gla)

    def both(i):
        def thunk():
            if i < len(proj):
                proj[i]()
            if i < len(gla):
                gla[i]()
        return thunk

    fillers = [both(i) for i in range(max(len(proj), len(gla)))]

    first = jnp.where(qi == 0, SB_NEG_BIG, 0.0).astype(F32)
    blocks = []
    for h in range(heads):
        blocks += [(h, kc_ref, vc_ref, True, None, None), (h, kp_ref, vp_ref, False, 2 * h, first)]
    carries, outs = run_blocks(blocks, fillers)
    for h in range(heads):
        carry_ref[h] = carries[2 * h + 1]
        acc_ref[:, hcols[h]] = outs[2 * h] + outs[2 * h + 1]

    def more(state_):
        kb, live = state_
        return jnp.logical_and(kb >= 0, live)

    def body(state_):
        kb, _ = state_
        r0 = pl.multiple_of(seq_row0 + kb * tq, tq)
        copies = [pltpu.make_async_copy(u_any.at[pl.ds(r0, tq), pl.ds(off, D_SB)], kv_ref.at[i], sem.at[i])
                  for i, off in enumerate((U_SK, U_SV))]
        for cp in copies:
            cp.start()
        for cp in copies:
            cp.wait()
        cs, os_ = run_blocks([(h, kv_ref.at[0], kv_ref.at[1], False, None, carry_ref[h]) for h in range(heads)])
        top = None
        for h in range(heads):
            carry_ref[h] = cs[h]
            acc_ref[:, hcols[h]] += os_[h]
            top = jnp.max(cs[h]) if top is None else jnp.maximum(top, jnp.max(cs[h]))
        return kb - 1, top > SB_LOG2_ZERO

    lax.while_loop(more, body, (qi - 2, jnp.max(carry_ref[...]) > SB_LOG2_ZERO))
    gate = sgate_ref[...].astype(F32)
    mix_ref[slot, :, D_GLA:] = (acc_ref[...] * (gate * _sigmoid(gate))).astype(BF16)


def _tile_call(u, log_a, g_head, x2, p2, w_out, g_post, w_gate, b_gate, w_proj, *, batch, tq):
    n, d = x2.shape
    dp = p2.shape[1]
    dmix = w_out.shape[0]
    seq = n // batch
    nq = seq // tq
    tiles = n // tq
    assert seq % tq == 0 and d % SB_PROJ_CHUNK == 0 and tq == GLA_CHUNK * GLA_GROUP, (seq, tq, d)
    assert dmix == D_GLA + D_SB, dmix
    j = np.arange(tq)[:, None]
    s = np.arange(tq)[None, :]
    tri = jnp.asarray((j > s).astype(np.float32), BF16)
    a_mat = jnp.asarray(_gla_decay_matrix(), BF16)
    lvl = jnp.asarray(_gla_level_matrix())
    cur = lambda t: jnp.minimum(t, tiles - 1)
    bef = lambda t: jnp.maximum(cur(t) - 1, 0)
    prv = lambda t: jnp.maximum(t - 1, 0)
    const = lambda t: (0, 0)
    once = pl.Buffered(1)
    ucol = lambda off, width: off // width
    return pl.pallas_call(
        functools.partial(_tile_kernel, tq=tq, nq=nq, tiles=tiles),
        grid=(tiles + 1,),
        in_specs=[
            pl.BlockSpec((tq, D_SB), lambda t: (cur(t), ucol(U_SQ, D_SB))),
            pl.BlockSpec((tq, D_SB), lambda t: (cur(t), ucol(U_SK, D_SB))),
            pl.BlockSpec((tq, D_SB), lambda t: (cur(t), ucol(U_SV, D_SB))),
            pl.BlockSpec((tq, D_SB), lambda t: (bef(t), ucol(U_SK, D_SB))),
            pl.BlockSpec((tq, D_SB), lambda t: (bef(t), ucol(U_SV, D_SB))),
            pl.BlockSpec((tq, D_SB), lambda t: (cur(t), ucol(U_SG, D_SB))),
            pl.BlockSpec((tq, tq), const),
            pl.BlockSpec((tq, D_QK), lambda t: (cur(t), ucol(U_GQ, D_QK))),
            pl.BlockSpec((tq, D_QK), lambda t: (cur(t), ucol(U_GK, D_QK))),
            pl.BlockSpec((tq, D_GLA), lambda t: (cur(t), ucol(U_GV, D_GLA))),
            pl.BlockSpec((tq, D_GLA), lambda t: (cur(t), ucol(U_GG, D_GLA))),
            pl.BlockSpec((tq, D_QK), lambda t: (cur(t), 0)),
            pl.BlockSpec(a_mat.shape, const),
            pl.BlockSpec(lvl.shape, const),
            pl.BlockSpec((1, GLA_DV), const),
            pl.BlockSpec((tq, d), lambda t: (prv(t), 0)),
            pl.BlockSpec((tq, dp), lambda t: (prv(t), 0)),
            pl.BlockSpec((dmix, d), const, pipeline_mode=once),
            pl.BlockSpec((1, d), const),
            pl.BlockSpec((d, d), const, pipeline_mode=once),
            pl.BlockSpec((1, d), const),
            pl.BlockSpec((dp, d), const, pipeline_mode=once),
            pl.BlockSpec(memory_space=pl.ANY),
        ],
        out_specs=pl.BlockSpec((tq, d), lambda t: (prv(t), 0)),
        out_shape=jax.ShapeDtypeStruct((n, d), F32),
        scratch_shapes=[
            pltpu.VMEM((SB_HEADS, tq, 1), F32),
            pltpu.VMEM((tq, D_SB), F32),
            pltpu.VMEM((2, tq, D_GLA + D_SB), BF16),
            pltpu.VMEM((GLA_HEADS, GLA_DV, GLA_DK), F32),
            pltpu.VMEM((2, tq, D_SB), BF16),
            pltpu.SemaphoreType.DMA((2,)),
        ],
        compiler_params=pltpu.CompilerParams(
            dimension_semantics=("arbitrary",),
            vmem_limit_bytes=VMEM_LIMIT_BYTES),
        name="mixer_out",
    )(u, u, u, u, u, u, tri, u, u, u, u, log_a, a_mat, lvl, g_head,
      x2, p2, w_out, g_post, w_gate, b_gate, w_proj, u)


def kernel(x, p, g_pre, w_in, w_a2, b_a, g_gla_head, w_out, g_post, w_ple_gate, b_ple_gate, w_ple_proj):
    batch, seq, d = x.shape
    n = batch * seq
    depth = w_in.shape[0]
    lr0 = U_SQ
    h_res = x.astype(F32).reshape(n, d)
    for i in range(depth):
        wt = jnp.transpose(w_in[i])
        wt_main = _w_prep(wt, tc=256)
        wt_lr = jnp.pad(wt[lr0:lr0 + GLA_RANK], ((0, LANES - GLA_RANK), (0, 0))).astype(BF16)
        w_a2p = jnp.pad(w_a2[i], ((0, LANES - GLA_RANK), (0, 0))).astype(BF16)
        u, log_a = _in_proj(h_res, g_pre[i][None, :], wt_main, wt_lr, w_a2p, b_a[i][None, :],
                            tm=1024, tn=1792)
        h_res = _tile_call(u, log_a, g_gla_head[i][None, :], h_res, p[i].astype(F32).reshape(n, -1),
                           w_out[i].astype(BF16), g_post[i][None, :],
                           w_ple_gate[i].astype(BF16), b_ple_gate[i][None, :],
                           w_ple_proj[i].astype(BF16), batch=batch, tq=256)
    return h_res.reshape(batch, seq, d).astype(x.dtype)
```
